```python
import jax, jax.numpy as jnp
from jax import lax
import numpy as np

D_MODEL = 2048
BATCH = 4
SEQ = 2048
DEPTH = 1
DEC_BATCH = 128
DEC_SEQ = 1
PAST_LEN = 16384
PAGE_SIZE = 128

D_PLE = 256
D_A = D_MODEL // 2
HD_A = 64
H_A = D_A // HD_A
LORA_W = 64
LORA_A = 64
D_B = D_MODEL - D_A
H_B = 4
HD_B = D_B // H_B
CHUNK = 128
ROPE_BASE = 10000.0
RMS_EPS = 1e-6
GN_EPS_A = HD_A * 1e-5
GN_EPS_B = 1e-5

A_SHIFT = 3 * D_A + LORA_W + LORA_A
OFF_GA = A_SHIFT
OFF_B = A_SHIFT + D_A
N_COLS = OFF_B + 4 * D_B

kernel_name = "hybrid_rwkv7_retention_decode_step"


def rmsnorm(x, g):
    xf = x.astype(jnp.float32)
    y = xf * lax.rsqrt(jnp.mean(xf * xf, axis=-1, keepdims=True) + RMS_EPS)
    return (y * g.astype(jnp.float32)).astype(x.dtype)


def head_norm(x, g, b, eps):
    mu = jnp.mean(x, axis=-1, keepdims=True)
    xc = x - mu
    var = jnp.mean(xc * xc, axis=-1, keepdims=True)
    y = (xc * lax.rsqrt(var + eps)).reshape(x.shape[:-2] + (-1,))
    return y * g.astype(jnp.float32) + b.astype(jnp.float32)


def rotary_every_two(x, pos):
    d = x.shape[-1]
    angle = 1.0 / (ROPE_BASE ** jnp.linspace(0.0, 1.0, d // 2, dtype=jnp.float32))
    theta = pos[:, None] * angle[None, :]
    cos = jnp.cos(theta)[None, :, None, :]
    sin = jnp.sin(theta)[None, :, None, :]
    xp = x.reshape(x.shape[:-1] + (d // 2, 2))
    x1, x2 = xp[..., 0], xp[..., 1]
    out = jnp.stack([x1 * cos - x2 * sin, x1 * sin + x2 * cos], axis=-1)
    return out.reshape(x.shape)


def rwkv7_scan(S0, r, w, k, v, a_vec, b_vec):
    def step(S, inp):
        r_t, w_t, k_t, v_t, a_t, b_t = inp
        sa = jnp.einsum('bhvk,bhk->bhv', S, a_t)
        S = S * w_t[:, :, None, :] + sa[..., None] * b_t[:, :, None, :] + v_t[..., None] * k_t[:, :, None, :]
        y = jnp.einsum('bhvk,bhk->bhv', S, r_t)
        return S, y
    xs = tuple(jnp.swapaxes(t, 0, 1) for t in (r, w, k, v, a_vec, b_vec))
    S_T, ys = lax.scan(step, S0, xs)
    return jnp.swapaxes(ys, 0, 1), S_T


def retention_chunk(S0, q, k, v, lg):
    L = q.shape[2]
    idx = jnp.arange(L, dtype=jnp.float32)
    diff = idx[:, None] - idx[None, :]
    dmask = jnp.where(diff >= 0, jnp.exp(jnp.maximum(diff, 0.0)[None] * lg[:, None, None]), 0.0)
    scores = jnp.einsum('bhid,bhjd->bhij', q, k) * dmask[None]
    inner = jnp.einsum('bhij,bhjv->bhiv', scores, v)
    cross = jnp.einsum('bhid,bhdv->bhiv', q, S0) * jnp.exp((idx + 1.0)[None, :] * lg[:, None])[None, :, :, None]
    kdec = jnp.exp((L - 1.0 - idx)[None, :] * lg[:, None])
    S_new = S0 * jnp.exp(L * lg)[None, :, None, None] + jnp.einsum('bhjd,bhjv,hj->bhdv', k, v, kdec)
    return inner + cross, S_new


def retention(S0, q, k, v, lg):
    B, H, T, _ = q.shape
    chunk = CHUNK if T % CHUNK == 0 else T
    nc = T // chunk
    def to_chunks(t):
        return jnp.moveaxis(t.reshape(B, H, nc, chunk, t.shape[-1]), 2, 0)
    def step(S, inp):
        qc, kc, vc = inp
        o, S = retention_chunk(S, qc, kc, vc, lg)
        return S, o
    S_T, os_ = lax.scan(step, S0, (to_chunks(q), to_chunks(k), to_chunks(v)))
    out = jnp.moveaxis(os_, 0, 2).reshape(B, H, T, v.shape[-1])
    return out, S_T


def hybrid_layer(h, p, pos, wkv0, shift0, ret0, g_ln, w_in, mu_shift, w0, w_wB, a0, w_aB, k_k, k_a, r_k,
                 gn_a_g, gn_a_b, gn_b_g, gn_b_b, w_out, w_ple, w_ple_gate):
    B, T, _ = h.shape
    f32 = jnp.float32
    u = rmsnorm(h, g_ln)
    z = u @ w_in

    feats = z[..., :A_SHIFT]
    prev = jnp.concatenate([shift0[:, None].astype(feats.dtype), feats[:, :-1]], axis=1)
    xs = (feats + mu_shift * (prev - feats)).astype(f32)
    r = xs[..., :D_A]
    k = xs[..., D_A:2 * D_A]
    v = xs[..., 2 * D_A:3 * D_A]
    w_lo = xs[..., 3 * D_A:3 * D_A + LORA_W]
    a_lo = xs[..., 3 * D_A + LORA_W:]
    w_log = -jax.nn.softplus(-(w0.astype(f32) + jnp.tanh(w_lo) @ w_wB.astype(f32))) - 0.5
    decay = jnp.exp(-jnp.exp(w_log))
    a = jax.nn.sigmoid(a0.astype(f32) + a_lo @ w_aB.astype(f32))
    hs = lambda t: t.reshape(B, T, H_A, HD_A)
    kk = hs(k * k_k.astype(f32))
    kk = kk / jnp.maximum(jnp.sqrt(jnp.sum(kk * kk, axis=-1, keepdims=True)), 1e-12)
    k = k * (1.0 + (a - 1.0) * k_a.astype(f32))
    r4, k4, v4, a4 = hs(r), hs(k), hs(v), hs(a)
    y_a, wkv_T = rwkv7_scan(wkv0.astype(f32), r4, hs(decay), k4, v4, -kk, kk * a4)
    bonus = (jnp.sum(r4 * k4 * r_k.astype(f32), axis=-1, keepdims=True) * v4).reshape(B, T, D_A)
    y_a = (head_norm(y_a, gn_a_g, gn_a_b, GN_EPS_A) + bonus) * jax.nn.silu(z[..., OFF_GA:OFF_GA + D_A].astype(f32))

    hb = lambda t: t.astype(f32).reshape(B, T, H_B, HD_B)
    qb = rotary_every_two(hb(z[..., OFF_B:OFF_B + D_B]), pos)
    kb = rotary_every_two(hb(z[..., OFF_B + D_B:OFF_B + 2 * D_B]), pos) * (HD_B ** -0.5)
    vb = hb(z[..., OFF_B + 2 * D_B:OFF_B + 3 * D_B])
    gb = z[..., OFF_B + 3 * D_B:].astype(f32)
    lg = jnp.log(1.0 - jnp.exp2(-5.0 - jnp.arange(H_B, dtype=f32)))
    to_bh = lambda t: jnp.transpose(t, (0, 2, 1, 3))
    y_b, ret_T = retention(ret0.astype(f32), to_bh(qb), to_bh(kb), to_bh(vb), lg)
    y_b = head_norm(to_bh(y_b), gn_b_g, gn_b_b, GN_EPS_B) * jax.nn.silu(gb)

    mix = jnp.concatenate([y_a, y_b], axis=-1).astype(h.dtype)
    h = h + mix @ w_out
    h = h + jax.nn.sigmoid(h @ w_ple_gate) * (p @ w_ple)
    return h, wkv_T, feats[:, -1], ret_T


def setup_inputs(seed: int = 0) -> dict:
    key = jax.random.key(seed)
    ks = jax.random.split(key, 32)
    f32 = jnp.float32
    nrm = lambda k, s, sc: jax.random.normal(k, s, f32) * sc
    return {
        "x_prompt": nrm(ks[0], (BATCH, SEQ, D_MODEL), 1.0),
        "x_sample": nrm(ks[1], (DEC_BATCH, DEC_SEQ, D_MODEL), 1.0),
        "p_prompt": nrm(ks[2], (DEPTH, BATCH, SEQ, D_PLE), 1.0),
        "p_sample": nrm(ks[3], (DEPTH, DEC_BATCH, DEC_SEQ, D_PLE), 1.0),
        "state_wkv": nrm(ks[4], (DEPTH, DEC_BATCH, H_A, HD_A, HD_A), 0.5),
        "state_shift": nrm(ks[5], (DEPTH, DEC_BATCH, A_SHIFT), 1.0),
        "state_ret": nrm(ks[6], (DEPTH, DEC_BATCH, H_B, HD_B, HD_B), 1.0),
        "g_ln": 1.0 + nrm(ks[7], (DEPTH, D_MODEL), 0.02),
        "w_in": nrm(ks[8], (DEPTH, D_MODEL, N_COLS), D_MODEL ** -0.5),
        "mu_shift": jax.random.uniform(ks[9], (DEPTH, A_SHIFT), f32, 0.0, 1.0),
        "w0": jax.random.uniform(ks[10], (DEPTH, D_A), f32, -5.0, 0.0),
        "w_wB": nrm(ks[11], (DEPTH, LORA_W, D_A), 0.1 * LORA_W ** -0.5),
        "a0": nrm(ks[12], (DEPTH, D_A), 0.1),
        "w_aB": nrm(ks[13], (DEPTH, LORA_A, D_A), 0.1 * LORA_A ** -0.5),
        "k_k": 0.85 + nrm(ks[14], (DEPTH, D_A), 0.02),
        "k_a": 1.0 + nrm(ks[15], (DEPTH, D_A), 0.02),
        "r_k": nrm(ks[16], (DEPTH, H_A, HD_A), 0.1),
        "gn_a_g": 1.0 + nrm(ks[17], (DEPTH, D_A), 0.02),
        "gn_a_b": nrm(ks[18], (DEPTH, D_A), 0.02),
        "gn_b_g": 1.0 + nrm(ks[19], (DEPTH, D_B), 0.02),
        "gn_b_b": nrm(ks[20], (DEPTH, D_B), 0.02),
        "w_out": nrm(ks[21], (DEPTH, D_A + D_B, D_MODEL), (D_A + D_B) ** -0.5),
        "w_ple": nrm(ks[22], (DEPTH, D_PLE, D_MODEL), D_PLE ** -0.5),
        "w_ple_gate": nrm(ks[23], (DEPTH, D_MODEL, D_MODEL), D_MODEL ** -0.5),
        "g_final": 1.0 + nrm(ks[24], (D_MODEL,), 0.02),
    }


def reference(x_prompt, x_sample, p_prompt, p_sample, state_wkv, state_shift, state_ret, g_ln, w_in, mu_shift,
              w0, w_wB, a0, w_aB, k_k, k_a, r_k, gn_a_g, gn_a_b, gn_b_g, gn_b_b, w_out, w_ple, w_ple_gate, g_final):
    f32 = jnp.float32
    Bp, Tp, _ = x_prompt.shape
    Bs, Ts, _ = x_sample.shape
    pos_p = jnp.arange(Tp, dtype=f32)
    pos_s = (PAST_LEN + jnp.arange(Ts)).astype(f32)
    hp, hs_ = x_prompt, x_sample
    wkv_p, shift_p, ret_p, wkv_s, shift_s, ret_s = [], [], [], [], [], []
    for i in range(DEPTH):
        lw = (g_ln[i], w_in[i], mu_shift[i], w0[i], w_wB[i], a0[i], w_aB[i], k_k[i], k_a[i], r_k[i],
              gn_a_g[i], gn_a_b[i], gn_b_g[i], gn_b_b[i], w_out[i], w_ple[i], w_ple_gate[i])
        hp, s1, s2, s3 = hybrid_layer(hp, p_prompt[i], pos_p,
                                      jnp.zeros((Bp, H_A, HD_A, HD_A), f32),
                                      jnp.zeros((Bp, A_SHIFT), x_prompt.dtype),
                                      jnp.zeros((Bp, H_B, HD_B, HD_B), f32), *lw)
        wkv_p.append(s1); shift_p.append(s2); ret_p.append(s3)
        hs_, s1, s2, s3 = hybrid_layer(hs_, p_sample[i], pos_s, state_wkv[i], state_shift[i], state_ret[i], *lw)
        wkv_s.append(s1); shift_s.append(s2); ret_s.append(s3)
    y_prompt = rmsnorm(hp, g_final)
    y_sample = rmsnorm(hs_, g_final)
    return (y_prompt, y_sample, jnp.stack(wkv_p), jnp.stack(shift_p), jnp.stack(ret_p),
            jnp.stack(wkv_s), jnp.stack(shift_s), jnp.stack(ret_s))
```

```python
import functools

import jax
import jax.numpy as jnp
from jax import lax
from jax.experimental import pallas as pl
from jax.experimental.pallas import tpu as pltpu

F32 = jnp.float32
BF16 = jnp.bfloat16

HD_A = 64
LORA = 64
H_B = 4
ROPE_BASE = 10000.0
RMS_EPS = 1e-6
GN_EPS_A = HD_A * 1e-5
GN_EPS_B = 1e-5
PAST_LEN = 16384

LANES = 128
GROUP = 256
HEADS_PER_GROUP = GROUP // HD_A
CHUNK_A = 64
STACK = HEADS_PER_GROUP * CHUNK_A
CHUNK_B = 128
VMEM_LIMIT = 48 * 1024 * 1024


def _cparams(sem):
    return pltpu.CompilerParams(dimension_semantics=sem, vmem_limit_bytes=VMEM_LIMIT)


def _bdot(a, b):
    return jnp.dot(a.astype(BF16), b.astype(BF16), preferred_element_type=F32)


def _bdot_nt(a, b):
    return lax.dot_general(a.astype(BF16), b.astype(BF16), (((1,), (1,)), ((), ())),
                           preferred_element_type=F32)


def _bdot_tn(a, b):
    return lax.dot_general(a.astype(BF16), b.astype(BF16), (((0,), (0,)), ((), ())),
                           preferred_element_type=F32)


def _split3(x):
    hi = x.astype(BF16)
    r1 = x - hi.astype(F32)
    mid = r1.astype(BF16)
    lo = (r1 - mid.astype(F32)).astype(BF16)
    return hi, mid, lo


def _dot_exact_rhs(x, m01):
    hi, mid, lo = _split3(x)
    d = lambda t: jnp.dot(t, m01, preferred_element_type=F32)
    return d(hi) + d(mid) + d(lo)


def _dot_exact_lhs(m01, x):
    hi, mid, lo = _split3(x)
    d = lambda t: jnp.dot(m01, t, preferred_element_type=F32)
    return d(hi) + d(mid) + d(lo)


def _seg_ones(n, seg):
    i = lax.broadcasted_iota(jnp.int32, (n, n), 0) // seg
    j = lax.broadcasted_iota(jnp.int32, (n, n), 1) // seg
    return jnp.where(i == j, 1.0, 0.0).astype(BF16)


def _sigmoid(x):
    return 1.0 / (1.0 + jnp.exp(-x))


def _silu(x):
    return x * _sigmoid(x)


def _softplus(x):
    return jnp.maximum(x, 0.0) + jnp.log(1.0 + jnp.exp(-jnp.abs(x)))


def _proj_kernel(x_ref, g_ref, w_ref, o_ref, u_ref):
    @pl.when(pl.program_id(1) == 0)
    def _():
        x = x_ref[...]
        ms = jnp.mean(x * x, axis=-1, keepdims=True)
        u_ref[...] = (x * lax.rsqrt(ms + RMS_EPS) * g_ref[...]).astype(BF16)

    o_ref[...] = jnp.dot(u_ref[...], w_ref[...], preferred_element_type=F32)


def _proj(x2d, g_ln, w_bf, tm, tn):
    m, d = x2d.shape
    nc = w_bf.shape[1]
    return pl.pallas_call(
        _proj_kernel,
        grid=(m // tm, nc // tn),
        in_specs=[pl.BlockSpec((tm, d), lambda i, j: (i, 0)),
                  pl.BlockSpec((1, d), lambda i, j: (0, 0)),
                  pl.BlockSpec((d, tn), lambda i, j: (0, j))],
        out_specs=pl.BlockSpec((tm, tn), lambda i, j: (i, j)),
        out_shape=jax.ShapeDtypeStruct((m, nc), F32),
        scratch_shapes=[pltpu.VMEM((tm, d), BF16)],
        compiler_params=_cparams(("parallel", "arbitrary")),
        name="in_proj",
    )(x2d, g_ln.reshape(1, d), w_bf)


def _out_kernel(x_ref, ya_ref, yb_ref, p_ref, woa_ref, wob_ref, wg_ref, wp_ref, gf_ref, o_ref):
    h2 = (x_ref[...]
          + jnp.dot(ya_ref[...], woa_ref[...], preferred_element_type=F32)
          + jnp.dot(yb_ref[...], wob_ref[...], preferred_element_type=F32))
    gate = _sigmoid(jnp.dot(h2.astype(BF16), wg_ref[...], preferred_element_type=F32))
    ple = jnp.dot(p_ref[...].astype(BF16), wp_ref[...], preferred_element_type=F32)
    h3 = h2 + gate * ple
    ms = jnp.mean(h3 * h3, axis=-1, keepdims=True)
    o_ref[...] = h3 * lax.rsqrt(ms + RMS_EPS) * gf_ref[...]


def _out_block(x2d, ya, yb, p2d, woa, wob, wg, wp, g_final, tm):
    m, d = x2d.shape
    da, db, dp = ya.shape[1], yb.shape[1], p2d.shape[1]
    row = lambda w: pl.BlockSpec((tm, w), lambda i: (i, 0))
    whole = lambda a: pl.BlockSpec(a.shape, lambda i: (0, 0), pipeline_mode=pl.Buffered(1))
    gf = g_final.reshape(1, d)
    return pl.pallas_call(
        _out_kernel,
        grid=(m // tm,),
        in_specs=[row(d), row(da), row(db), row(dp), whole(woa), whole(wob), whole(wg), whole(wp),
                  whole(gf)],
        out_specs=row(d),
        out_shape=jax.ShapeDtypeStruct((m, d), F32),
        compiler_params=_cparams(("parallel",)),
        name="out_block",
    )(x2d, ya, yb, p2d, woa, wob, wg, wp, gf)


def _rwkv_vectors(xr, xk, xv, xl, w0, a0, k_k, k_a, r_k, ww_pad, wa_pad, seg):
    lw = _bdot(jnp.tanh(xl), ww_pad)
    la = _bdot(xl, wa_pad)
    w_log = -_softplus(-(w0 + lw)) - 0.5
    logw = -jnp.exp(w_log)
    a = _sigmoid(a0 + la)
    kk = xk * k_k
    nrm = jnp.sqrt(_dot_exact_rhs(kk * kk, seg))
    kk = kk / jnp.maximum(nrm, 1e-12)
    k2 = xk * (1.0 + (a - 1.0) * k_a)
    bonus = _dot_exact_rhs(xr * k2 * r_k, seg) * xv
    return xr, k2, xv, logw, kk, a, bonus


def _rwkv_post(y, bonus, ga, gn_g, gn_b, seg):
    inv = 1.0 / HD_A
    mu = _dot_exact_rhs(y, seg) * inv
    yc = y - mu
    var = _dot_exact_rhs(yc * yc, seg) * inv
    hn = yc * lax.rsqrt(var + GN_EPS_A) * gn_g + gn_b
    return (hn + bonus) * _silu(ga)


def _rwkv_prompt_kernel(zr, zk, zv, zg, zl, pr, pk, pv, plr, sr, sk, sv, slr,
                        mur, muk, muv, mul, w0, a0, k_k, k_a, r_k, gng, gnb, ww, wa,
                        y_ref, s_ref,
                        S_scr, r_s, k_s, v_s, lw_s, kk_s, a_s, y_s):
    t = pl.program_id(2)
    nt = pl.num_programs(2)
    tb = zr.shape[0]
    nchunk = tb // CHUNK_A
    seg = _seg_ones(GROUP, HD_A)

    @pl.when(t == 0)
    def _():
        S_scr[...] = jnp.zeros_like(S_scr)

    row = lax.broadcasted_iota(jnp.int32, (tb, 1), 0)

    def shifted(z_ref, p_ref, s0_ref, mu_ref):
        x = z_ref[...]
        last = jnp.where(t == 0, s0_ref[...], p_ref[7:8, :])
        prev = jnp.where(row == 0, last, pltpu.roll(x, 1, 0))
        return x + mu_ref[...] * (prev - x)

    xr = shifted(zr, pr, sr, mur)
    xk = shifted(zk, pk, sk, muk)
    xv = shifted(zv, pv, sv, muv)
    xl = shifted(zl, plr, slr, mul)
    r, k2, v, logw, kk, a, bonus = _rwkv_vectors(
        xr, xk, xv, xl, w0[...], a0[...], k_k[...], k_a[...], r_k[...], ww[...], wa[...], seg)
    r_s[...] = r
    k_s[...] = k2
    v_s[...] = v
    lw_s[...] = logw
    kk_s[...] = kk
    a_s[...] = a

    lane = lax.broadcasted_iota(jnp.int32, (1, GROUP), 1) // HD_A
    masks = [jnp.where(lane == h, 1.0, 0.0) for h in range(HEADS_PER_GROUP)]
    ci = lax.broadcasted_iota(jnp.int32, (CHUNK_A, CHUNK_A), 0)
    cj = lax.broadcasted_iota(jnp.int32, (CHUNK_A, CHUNK_A), 1)
    tri_incl = jnp.where(ci >= cj, 1.0, 0.0).astype(BF16)
    si = lax.broadcasted_iota(jnp.int32, (STACK, STACK), 0)
    sj = lax.broadcasted_iota(jnp.int32, (STACK, STACK), 1)
    same = (si // CHUNK_A) == (sj // CHUNK_A)
    strict = same & ((si % CHUNK_A) > (sj % CHUNK_A))
    incl = same & ((si % CHUNK_A) >= (sj % CHUNK_A))

    def stack(x):
        return jnp.concatenate([x * m for m in masks], axis=0)

    def chunk(j, carry):
        sl = pl.ds(pl.multiple_of(j * CHUNK_A, CHUNK_A), CHUNK_A)
        rc, kc, vc, lwc, kkc, ac = r_s[sl, :], k_s[sl, :], v_s[sl, :], lw_s[sl, :], kk_s[sl, :], a_s[sl, :]
        cum = _dot_exact_lhs(tri_incl, lwc)
        e_pos = jnp.exp(cum)
        e_neg = jnp.exp(-cum)
        cum_end = cum[CHUNK_A - 1:CHUNK_A, :]
        e_tail = jnp.exp(cum_end - cum)
        beta = kkc * ac
        a_st = stack(-kkc * jnp.exp(cum - lwc))
        r_st = stack(rc * e_pos)
        b_st = stack(beta * e_neg)
        k_st = stack(kc * e_neg)
        v_st = stack(vc)
        bt_st = stack(beta * e_tail)
        kt_st = stack(kc * e_tail)

        S = S_scr[...]
        n_ab = jnp.where(strict, _bdot_nt(a_st, b_st), 0.0)
        n_ak = jnp.where(strict, _bdot_nt(a_st, k_st), 0.0)
        w_rb = jnp.where(incl, _bdot_nt(r_st, b_st), 0.0)
        w_rk = jnp.where(incl, _bdot_nt(r_st, k_st), 0.0)

        u = _bdot_nt(a_st, S) + _bdot(n_ak, v_st)
        npow = n_ab
        steps = CHUNK_A.bit_length() - 1
        for i in range(steps):
            u = u + _bdot(npow, u)
            if i + 1 < steps:
                npow = _bdot(npow, npow)

        y_st = _bdot_nt(r_st, S) + _bdot(w_rb, u) + _bdot(w_rk, v_st)
        y = y_st[0:CHUNK_A]
        for h in range(1, HEADS_PER_GROUP):
            y = y + y_st[h * CHUNK_A:(h + 1) * CHUNK_A]
        y_s[sl, :] = y
        S_scr[...] = S * jnp.exp(cum_end) + _bdot_tn(u, bt_st) + _bdot_tn(v_st, kt_st)
        return carry

    lax.fori_loop(0, nchunk, chunk, 0)

    y_ref[...] = _rwkv_post(y_s[...], bonus, zg[...], gng[...], gnb[...], seg).astype(y_ref.dtype)

    @pl.when(t == nt - 1)
    def _():
        s_ref[...] = S_scr[...]


def _rwkv_prompt(z2, sh_rkv, sh_l, mu_rkv, mu_l, w0, a0, k_k, k_a, r_k, gng, gnb, ww_pad, wa_pad, tb):
    b, t, nc = z2.shape
    da = w0.shape[1]
    ng = da // GROUP
    nt = t // tb
    lora_blk = (nc - LANES) // LANES
    per8 = tb // 8

    def zspec(off):
        return pl.BlockSpec((None, tb, GROUP), lambda bi, g, ti, off=off: (bi, ti, off + g))

    def pspec(off):
        return pl.BlockSpec((None, 8, GROUP),
                            lambda bi, g, ti, off=off: (bi, jnp.maximum(ti * per8 - 1, 0), off + g))

    def sspec(off):
        return pl.BlockSpec((None, 1, GROUP), lambda bi, g, ti, off=off: (bi, 0, off + g))

    def vspec(off=0):
        return pl.BlockSpec((1, GROUP), lambda bi, g, ti, off=off: (0, off + g))

    zl_spec = pl.BlockSpec((None, tb, LANES), lambda bi, g, ti: (bi, ti, lora_blk))
    pl_spec = pl.BlockSpec((None, 8, LANES),
                           lambda bi, g, ti: (bi, jnp.maximum(ti * per8 - 1, 0), lora_blk))
    sl_spec = pl.BlockSpec((None, 1, LANES), lambda bi, g, ti: (bi, 0, 0))
    ml_spec = pl.BlockSpec((1, LANES), lambda bi, g, ti: (0, 0))
    w_spec = pl.BlockSpec((LANES, GROUP), lambda bi, g, ti: (0, g))

    in_specs = [zspec(0), zspec(ng), zspec(2 * ng), zspec(3 * ng), zl_spec,
                pspec(0), pspec(ng), pspec(2 * ng), pl_spec,
                sspec(0), sspec(ng), sspec(2 * ng), sl_spec,
                vspec(0), vspec(ng), vspec(2 * ng), ml_spec,
                vspec(), vspec(), vspec(), vspec(), vspec(), vspec(), vspec(), w_spec, w_spec]
    blk = lambda: pltpu.VMEM((tb, GROUP), F32)
    return pl.pallas_call(
        _rwkv_prompt_kernel,
        grid=(b, ng, nt),
        in_specs=in_specs,
        out_specs=[pl.BlockSpec((None, tb, GROUP), lambda bi, g, ti: (bi, ti, g)),
                   pl.BlockSpec((None, None, GROUP, GROUP), lambda bi, g, ti: (bi, g, 0, 0))],
        out_shape=[jax.ShapeDtypeStruct((b, t, da), BF16),
                   jax.ShapeDtypeStruct((b, ng, GROUP, GROUP), F32)],
        scratch_shapes=[pltpu.VMEM((GROUP, GROUP), F32)] + [blk() for _ in range(7)],
        compiler_params=_cparams(("parallel", "parallel", "arbitrary")),
        name="rwkv_prompt",
    )(z2, z2, z2, z2, z2, z2, z2, z2, z2, sh_rkv, sh_rkv, sh_rkv, sh_l,
      mu_rkv, mu_rkv, mu_rkv, mu_l, w0, a0, k_k, k_a, r_k, gng, gnb, ww_pad, wa_pad)


def _rwkv_dec_pre_kernel(zr, zk, zv, zg, zl, pr, pk, pv, plr, mur, muk, muv, mul,
                         w0, a0, k_k, k_a, r_k, ww, wa,
                         r_o, w_o, k_o, v_o, al_o, be_o, bo_o, sg_o):
    seg = _seg_ones(zr.shape[1], HD_A)
    sh = lambda z, p, mu: z[...] + mu[...] * (p[...] - z[...])
    xr, xk, xv, xl = sh(zr, pr, mur), sh(zk, pk, muk), sh(zv, pv, muv), sh(zl, plr, mul)
    r, k2, v, logw, kk, a, bonus = _rwkv_vectors(
        xr, xk, xv, xl, w0[...], a0[...], k_k[...], k_a[...], r_k[...], ww[...], wa[...], seg)
    r_o[...] = r
    w_o[...] = jnp.exp(logw)
    k_o[...] = k2
    v_o[...] = v
    al_o[...] = -kk
    be_o[...] = kk * a
    bo_o[...] = bonus
    sg_o[...] = _silu(zg[...])


def _rwkv_dec_pre(z2, prev_rkv, prev_l, mu_rkv, mu_l, w0, a0, k_k, k_a, r_k, ww_pad, wa_pad):
    b, nc = z2.shape
    da = w0.shape[1]
    nb = da // LANES
    c = lambda w, j: pl.BlockSpec((b, w), lambda i, j=j: (0, j))
    in_specs = [c(da, 0), c(da, 1), c(da, 2), c(da, 3), c(LANES, (nc - LANES) // LANES),
                c(da, 0), c(da, 1), c(da, 2), c(LANES, 0),
                pl.BlockSpec((1, da), lambda i: (0, 0)), pl.BlockSpec((1, da), lambda i: (0, 1)),
                pl.BlockSpec((1, da), lambda i: (0, 2)), pl.BlockSpec((1, LANES), lambda i: (0, 0))]
    in_specs += [pl.BlockSpec((1, da), lambda i: (0, 0))] * 5
    in_specs += [pl.BlockSpec((LANES, da), lambda i: (0, 0))] * 2
    del nb
    return pl.pallas_call(
        _rwkv_dec_pre_kernel,
        grid=(1,),
        in_specs=in_specs,
        out_specs=[pl.BlockSpec((b, da), lambda i: (0, 0))] * 8,
        out_shape=[jax.ShapeDtypeStruct((b, da), F32)] * 8,
        compiler_params=_cparams(("arbitrary",)),
        name="rwkv_dec_pre",
    )(z2, z2, z2, z2, z2, prev_rkv, prev_rkv, prev_rkv, prev_l, mu_rkv, mu_rkv, mu_rkv, mu_l,
      w0, a0, k_k, k_a, r_k, ww_pad, wa_pad)


def _rwkv_dec_state_kernel(s_ref, w_ref, al_ref, be_ref, k_ref, r_ref, v_ref, so_ref, y_ref):
    S = s_ref[...]
    sa = jnp.sum(S * al_ref[...], axis=-1, keepdims=True)
    sn = S * w_ref[...] + sa * be_ref[...] + v_ref[...] * k_ref[...]
    so_ref[...] = sn
    y_ref[...] = jnp.sum(sn * r_ref[...], axis=-1, keepdims=True)


def _rwkv_dec_state(s0, w, al, be, k2, r, v, bb):
    b, h, n, _ = s0.shape
    rows = lambda x: x.reshape(b, h, 1, n)
    sspec = pl.BlockSpec((bb, h, n, n), lambda i: (i, 0, 0, 0))
    rspec = pl.BlockSpec((bb, h, 1, n), lambda i: (i, 0, 0, 0))
    cspec = pl.BlockSpec((bb, h, n, 1), lambda i: (i, 0, 0, 0))
    return pl.pallas_call(
        _rwkv_dec_state_kernel,
        grid=(b // bb,),
        in_specs=[sspec, rspec, rspec, rspec, rspec, rspec, cspec],
        out_specs=[sspec, cspec],
        out_shape=[jax.ShapeDtypeStruct(s0.shape, F32), jax.ShapeDtypeStruct((b, h, n, 1), F32)],
        compiler_params=_cparams(("parallel",)),
        name="rwkv_dec_state",
    )(s0, rows(w), rows(al), rows(be), rows(k2), rows(r), v.reshape(b, h, n, 1))


def _rotate_pairs(x, cos, sin_signed):
    n = x.shape[-1]
    lane = lax.broadcasted_iota(jnp.int32, x.shape, x.ndim - 1)
    nxt = pltpu.roll(x, n - 1, x.ndim - 1)
    prv = pltpu.roll(x, 1, x.ndim - 1)
    swapped = jnp.where(lane % 2 == 0, nxt, prv)
    return x * cos + swapped * sin_signed


def _head_norm_b(y, g, b):
    mu = jnp.mean(y, axis=-1, keepdims=True)
    yc = y - mu
    var = jnp.mean(yc * yc, axis=-1, keepdims=True)
    return yc * lax.rsqrt(var + GN_EPS_B) * g + b


def _ret_prompt_kernel(lg_ref, q_ref, k_ref, v_ref, g_ref, cos_ref, sin_ref, gng, gnb,
                       y_ref, s_ref, S_scr):
    c = pl.program_id(2)
    nc = pl.num_programs(2)
    L, d = q_ref.shape

    @pl.when(c == 0)
    def _():
        S_scr[...] = jnp.zeros_like(S_scr)

    lg = lg_ref[0:1, 0:1]
    cos, sin = cos_ref[...], sin_ref[...]
    q = _rotate_pairs(q_ref[...], cos, sin)
    k = _rotate_pairs(k_ref[...], cos, sin) * (d ** -0.5)
    v = v_ref[...]
    ii = lax.broadcasted_iota(jnp.int32, (L, L), 0)
    jj = lax.broadcasted_iota(jnp.int32, (L, L), 1)
    diff = (ii - jj).astype(F32)
    dmask = jnp.where(diff >= 0, jnp.exp(jnp.maximum(diff, 0.0) * lg), 0.0)
    idx = lax.broadcasted_iota(jnp.int32, (L, 1), 0).astype(F32)
    S = S_scr[...]
    scores = _bdot_nt(q, k) * dmask
    inner = _bdot(scores, v)
    cross = _bdot(q, S) * jnp.exp((idx + 1.0) * lg)
    kdec = jnp.exp((L - 1.0 - idx) * lg)
    S_new = S * jnp.exp(L * lg) + _bdot_tn(k * kdec, v)
    S_scr[...] = S_new
    y = _head_norm_b(inner + cross, gng[...], gnb[...]) * _silu(g_ref[...])
    y_ref[...] = y.astype(y_ref.dtype)

    @pl.when(c == nc - 1)
    def _():
        s_ref[...] = S_new


def _ret_prompt(z2, lg_tile, cos, sin, gng, gnb, col0):
    b, t, _ = z2.shape
    d = cos.shape[1]
    nh = gng.shape[1] // d
    L = CHUNK_B if t % CHUNK_B == 0 else t
    nchunk = t // L
    c0 = col0 // d
    zspec = lambda off: pl.BlockSpec((None, L, d), lambda bi, h, ci, off=off: (bi, ci, c0 + off + h))
    tspec = pl.BlockSpec((L, d), lambda bi, h, ci: (ci, 0))
    vspec = pl.BlockSpec((1, d), lambda bi, h, ci: (0, h))
    return pl.pallas_call(
        _ret_prompt_kernel,
        grid=(b, nh, nchunk),
        in_specs=[pl.BlockSpec((None, 8, LANES), lambda bi, h, ci: (h, 0, 0)),
                  zspec(0), zspec(nh), zspec(2 * nh), zspec(3 * nh), tspec, tspec, vspec, vspec],
        out_specs=[pl.BlockSpec((None, L, d), lambda bi, h, ci: (bi, ci, h)),
                   pl.BlockSpec((None, None, d, d), lambda bi, h, ci: (bi, h, 0, 0))],
        out_shape=[jax.ShapeDtypeStruct((b, t, nh * d), BF16),
                   jax.ShapeDtypeStruct((b, nh, d, d), F32)],
        scratch_shapes=[pltpu.VMEM((d, d), F32)],
        compiler_params=_cparams(("parallel", "parallel", "arbitrary")),
        name="ret_prompt",
    )(lg_tile, z2, z2, z2, z2, cos, sin, gng, gnb)


def _ret_dec_pre_kernel(zq, zk, zv, zg, cos_ref, sin_ref, q_o, k_o, in_o, sg_o):
    d = cos_ref.shape[1] // H_B
    seg = _seg_ones(zq.shape[1], d)
    cos, sin = cos_ref[...], sin_ref[...]
    q = _rotate_pairs(zq[...], cos, sin)
    k = _rotate_pairs(zk[...], cos, sin) * (d ** -0.5)
    q_o[...] = q
    k_o[...] = k
    in_o[...] = _dot_exact_rhs(q * k, seg) * zv[...]
    sg_o[...] = _silu(zg[...])


def _ret_dec_pre(z2, cos, sin, col0):
    b, _ = z2.shape
    db = cos.shape[1]
    c0 = col0 // db
    c = lambda j: pl.BlockSpec((b, db), lambda i, j=j: (0, c0 + j))
    t = pl.BlockSpec((1, db), lambda i: (0, 0))
    return pl.pallas_call(
        _ret_dec_pre_kernel,
        grid=(1,),
        in_specs=[c(0), c(1), c(2), c(3), t, t],
        out_specs=[pl.BlockSpec((b, db), lambda i: (0, 0))] * 4,
        out_shape=[jax.ShapeDtypeStruct((b, db), F32)] * 4,
        compiler_params=_cparams(("arbitrary",)),
        name="ret_dec_pre",
    )(z2, z2, z2, z2, cos, sin)


def _ret_dec_state_kernel(lg_ref, s_ref, q_ref, k_ref, v_ref, so_ref, cr_ref):
    nh = s_ref.shape[1]
    for h in range(nh):
        lg = lg_ref[h, 0:1, 0:1]
        gamma = jnp.exp(lg)
        S = s_ref[:, h]
        cr_ref[:, h] = jnp.sum(S * q_ref[:, h], axis=1, keepdims=True) * gamma
        so_ref[:, h] = S * gamma + k_ref[:, h] * v_ref[:, h]


def _ret_dec_state(lg_tile, s0, q, k, v, bb):
    b, nh, d, _ = s0.shape
    sspec = pl.BlockSpec((bb, nh, d, d), lambda i: (i, 0, 0, 0))
    cspec = pl.BlockSpec((bb, nh, d, 1), lambda i: (i, 0, 0, 0))
    rspec = pl.BlockSpec((bb, nh, 1, d), lambda i: (i, 0, 0, 0))
    return pl.pallas_call(
        _ret_dec_state_kernel,
        grid=(b // bb,),
        in_specs=[pl.BlockSpec(lg_tile.shape, lambda i: (0, 0, 0)), sspec, cspec, cspec, rspec],
        out_specs=[sspec, rspec],
        out_shape=[jax.ShapeDtypeStruct(s0.shape, F32), jax.ShapeDtypeStruct((b, nh, 1, d), F32)],
        compiler_params=_cparams(("parallel",)),
        name="ret_dec_state",
    )(lg_tile, s0, q.reshape(b, nh, d, 1), k.reshape(b, nh, d, 1), v.reshape(b, nh, 1, d))


def _dec_post_kernel(ya_ref, bo_ref, sga_ref, gag, gab, in_ref, cr_ref, sgb_ref, gbg, gbb,
                     oa_ref, ob_ref):
    da = ya_ref.shape[1]
    db = in_ref.shape[1]
    d = db // H_B
    sega = _seg_ones(da, HD_A)
    segb = _seg_ones(db, d)
    y = ya_ref[...]
    mu = _dot_exact_rhs(y, sega) * (1.0 / HD_A)
    yc = y - mu
    var = _dot_exact_rhs(yc * yc, sega) * (1.0 / HD_A)
    hn = yc * lax.rsqrt(var + GN_EPS_A) * gag[...] + gab[...]
    oa_ref[...] = ((hn + bo_ref[...]) * sga_ref[...]).astype(oa_ref.dtype)
    y = in_ref[...] + cr_ref[...]
    mu = _dot_exact_rhs(y, segb) * (1.0 / d)
    yc = y - mu
    var = _dot_exact_rhs(yc * yc, segb) * (1.0 / d)
    hn = yc * lax.rsqrt(var + GN_EPS_B) * gbg[...] + gbb[...]
    ob_ref[...] = (hn * sgb_ref[...]).astype(ob_ref.dtype)


def _dec_post(ya, bonus, sga, gag, gab, inner, cross, sgb, gbg, gbb):
    b, da = ya.shape
    db = inner.shape[1]
    fa = pl.BlockSpec((b, da), lambda i: (0, 0))
    fb = pl.BlockSpec((b, db), lambda i: (0, 0))
    va = pl.BlockSpec((1, da), lambda i: (0, 0))
    vb = pl.BlockSpec((1, db), lambda i: (0, 0))
    return pl.pallas_call(
        _dec_post_kernel,
        grid=(1,),
        in_specs=[fa, fa, fa, va, va, fb, fb, fb, vb, vb],
        out_specs=[fa, fb],
        out_shape=[jax.ShapeDtypeStruct((b, da), BF16), jax.ShapeDtypeStruct((b, db), BF16)],
        compiler_params=_cparams(("arbitrary",)),
        name="dec_post",
    )(ya, bonus, sga, gag, gab, inner, cross, sgb, gbg, gbb)


def _rot_tables(pos, d):
    angle = 1.0 / (ROPE_BASE ** jnp.linspace(0.0, 1.0, d // 2, dtype=F32))
    theta = pos[:, None] * angle[None, :]
    cos = jnp.repeat(jnp.cos(theta), 2, axis=-1)
    sin = jnp.repeat(jnp.sin(theta), 2, axis=-1)
    sign = jnp.tile(jnp.array([-1.0, 1.0], F32), d // 2)
    return cos, sin * sign


def _diag_blocks(s_bd):
    b, g = s_bd.shape[:2]
    s6 = s_bd.reshape(b, g, HEADS_PER_GROUP, HD_A, HEADS_PER_GROUP, HD_A)
    blocks = jnp.stack([s6[:, :, i, :, i, :] for i in range(HEADS_PER_GROUP)], axis=2)
    return blocks.reshape(b, g * HEADS_PER_GROUP, HD_A, HD_A)


def _pick_tile(n, pref):
    for t in pref:
        if n % t == 0:
            return t
    return n


def kernel(x_prompt, x_sample, p_prompt, p_sample, state_wkv, state_shift, state_ret, g_ln, w_in,
           mu_shift, w0, w_wB, a0, w_aB, k_k, k_a, r_k, gn_a_g, gn_a_b, gn_b_g, gn_b_b, w_out, w_ple,
           w_ple_gate, g_final):
    depth = g_ln.shape[0]
    assert depth == 1
    bp, tp, d_model = x_prompt.shape
    bs, ts, _ = x_sample.shape
    assert ts == 1
    da = w0.shape[1]
    db = gn_b_g.shape[1]
    hd_b = db // H_B
    n_rkv = 3 * da
    a_shift = n_rkv + 2 * LORA
    assert LORA * 2 == LANES and a_shift == mu_shift.shape[1]
    assert tp % CHUNK_A == 0 and da % GROUP == 0

    w = w_in[0]
    w_perm = jnp.concatenate([w[:, :n_rkv], w[:, a_shift:], w[:, n_rkv:a_shift]], axis=1).astype(BF16)
    col_b = n_rkv + da
    mu = mu_shift[0]
    mu_rkv = mu[:n_rkv].reshape(1, n_rkv)
    mu_l = mu[n_rkv:].reshape(1, LANES)
    zpad = jnp.zeros((LORA, da), F32)
    ww_pad = jnp.concatenate([w_wB[0], zpad], axis=0).astype(BF16)
    wa_pad = jnp.concatenate([zpad, w_aB[0]], axis=0).astype(BF16)
    v1 = lambda a: a[0].reshape(1, -1)
    w0_, a0_, kk_, ka_, rk_ = v1(w0), v1(a0), v1(k_k), v1(k_a), v1(r_k)
    gag, gab, gbg, gbb = v1(gn_a_g), v1(gn_a_b), v1(gn_b_g), v1(gn_b_b)
    woa = w_out[0][:da].astype(BF16)
    wob = w_out[0][da:].astype(BF16)
    wg = w_ple_gate[0].astype(BF16)
    wp = w_ple[0].astype(BF16)
    lg = jnp.log(1.0 - jnp.exp2(-5.0 - jnp.arange(H_B, dtype=F32)))
    lg_tile = jnp.broadcast_to(lg[:, None, None], (H_B, 8, LANES))

    tn = _pick_tile(w_perm.shape[1], (640, 128))

    xp = x_prompt.reshape(bp * tp, d_model)
    zp = _proj(xp, g_ln[0], w_perm, _pick_tile(bp * tp, (1024, 512, 256, 128)), tn)
    zp3 = zp.reshape(bp, tp, -1)
    sh0 = jnp.zeros((bp, 1, a_shift), F32)
    tb = _pick_tile(tp, (256, 128, 64))
    ya_p, s_bd = _rwkv_prompt(zp3, sh0[:, :, :n_rkv], sh0[:, :, n_rkv:], mu_rkv, mu_l, w0_, a0_, kk_, ka_,
                              rk_, gag, gab, ww_pad, wa_pad, tb)
    cos_p, sin_p = _rot_tables(jnp.arange(tp, dtype=F32), hd_b)
    yb_p, ret_p = _ret_prompt(zp3, lg_tile, cos_p, sin_p, gbg, gbb, col_b)
    y_prompt = _out_block(xp, ya_p.reshape(bp * tp, da), yb_p.reshape(bp * tp, db),
                          p_prompt[0].reshape(bp * tp, -1), woa, wob, wg, wp, g_final,
                          _pick_tile(bp * tp, (256, 128))).reshape(bp, tp, d_model)
    wkv_p = _diag_blocks(s_bd)
    shift_p = jnp.concatenate([zp3[:, -1, :n_rkv], zp3[:, -1, -LANES:]], axis=-1)

    xs = x_sample.reshape(bs, d_model)
    zs = _proj(xs, g_ln[0], w_perm, bs, tn)
    prev = state_shift[0]
    r, wdec, k2, v, al, be, bonus, sga = _rwkv_dec_pre(
        zs, prev[:, :n_rkv], prev[:, n_rkv:], mu_rkv, mu_l, w0_, a0_, kk_, ka_, rk_, ww_pad, wa_pad)
    h_a = da // HD_A
    hs = lambda a: a.reshape(bs, h_a, HD_A)
    wkv_s, y_col = _rwkv_dec_state(state_wkv[0], hs(wdec), hs(al), hs(be), hs(k2), hs(r), hs(v),
                                   _pick_tile(bs, (4, 2, 1)))
    cos_s, sin_s = _rot_tables((PAST_LEN + jnp.arange(ts)).astype(F32), hd_b)
    cos_s = jnp.tile(cos_s, (1, H_B))
    sin_s = jnp.tile(sin_s, (1, H_B))
    q_s, k_s, inner, sgb = _ret_dec_pre(zs, cos_s, sin_s, col_b)
    hb = lambda a: a.reshape(bs, H_B, hd_b)
    ret_s, cross = _ret_dec_state(lg_tile, state_ret[0], hb(q_s), hb(k_s),
                                  hb(zs[:, col_b + 2 * db:col_b + 3 * db]), _pick_tile(bs, (4, 2, 1)))
    ya_s, yb_s = _dec_post(y_col.reshape(bs, da), bonus, sga, gag, gab, inner, cross.reshape(bs, db),
                           sgb, gbg, gbb)
    y_sample = _out_block(xs, ya_s, yb_s, p_sample[0].reshape(bs, -1), woa, wob, wg, wp, g_final,
                          bs).reshape(bs, 1, d_model)
    shift_s = jnp.concatenate([zs[:, :n_rkv], zs[:, -LANES:]], axis=-1)

    return (y_prompt, y_sample, wkv_p[None], shift_p[None], ret_p[None],
            wkv_s[None], shift_s[None], ret_s[None])
```

```python
import functools

import jax
import jax.numpy as jnp
from jax import lax
from jax.experimental import pallas as pl
from jax.experimental.pallas import tpu as pltpu

F32 = jnp.float32
BF16 = jnp.bfloat16

HD_A = 64
LORA = 64
H_B = 4
ROPE_BASE = 10000.0
RMS_EPS = 1e-6
GN_EPS_A = HD_A * 1e-5
GN_EPS_B = 1e-5
PAST_LEN = 16384
DECAY_SCALE = 0.6065306597126334

LANES = 128
GROUP = 256
HEADS_PER_GROUP = GROUP // HD_A
CHUNK_A = 64
STACK = HEADS_PER_GROUP * CHUNK_A
CHUNK_B = 128
VMEM_LIMIT = 48 * 1024 * 1024


def _cparams(sem):
    return pltpu.CompilerParams(dimension_semantics=sem, vmem_limit_bytes=VMEM_LIMIT)


def _bdot(a, b):
    return jnp.dot(a.astype(BF16), b.astype(BF16), preferred_element_type=F32)


def _bdot_nt(a, b):
    return lax.dot_general(a.astype(BF16), b.astype(BF16), (((1,), (1,)), ((), ())),
                           preferred_element_type=F32)


def _bdot_tn(a, b):
    return lax.dot_general(a.astype(BF16), b.astype(BF16), (((0,), (0,)), ((), ())),
                           preferred_element_type=F32)


def _split3(x):
    hi = x.astype(BF16)
    r1 = x - hi.astype(F32)
    mid = r1.astype(BF16)
    lo = (r1 - mid.astype(F32)).astype(BF16)
    return hi, mid, lo


def _dot_exact_rhs(x, m01):
    hi = x.astype(BF16)
    lo = (x - hi.astype(F32)).astype(BF16)
    d = lambda t: jnp.dot(t, m01, preferred_element_type=F32)
    return d(hi) + d(lo)


def _dot_exact_lhs(m01, x):
    hi, mid, lo = _split3(x)
    d = lambda t: jnp.dot(m01, t, preferred_element_type=F32)
    return d(hi) + d(mid) + d(lo)


def _segsum(x, seg01):
    n = seg01.shape[0]
    parts = [_dot_exact_rhs(x[:, i:i + n], seg01) for i in range(0, x.shape[1], n)]
    return parts[0] if len(parts) == 1 else jnp.concatenate(parts, axis=1)


def _seg_ones(n, seg):
    i = lax.broadcasted_iota(jnp.int32, (n, n), 0) // seg
    j = lax.broadcasted_iota(jnp.int32, (n, n), 1) // seg
    return jnp.where(i == j, 1.0, 0.0).astype(BF16)


def _sigmoid(x):
    return 1.0 / (1.0 + jnp.exp(-x))


def _silu(x):
    return x * _sigmoid(x)


def _proj_kernel(x_ref, g_ref, w_ref, o_ref, u_ref):
    @pl.when(pl.program_id(1) == 0)
    def _():
        x = x_ref[...]
        ms = jnp.mean(x * x, axis=-1, keepdims=True)
        u_ref[...] = (x * lax.rsqrt(ms + RMS_EPS) * g_ref[...]).astype(BF16)

    o_ref[...] = jnp.dot(u_ref[...], w_ref[...], preferred_element_type=F32)


def _proj(x2d, g_ln, w_bf, tm, tn):
    m, d = x2d.shape
    nc = w_bf.shape[1]
    return pl.pallas_call(
        _proj_kernel,
        grid=(m // tm, nc // tn),
        in_specs=[pl.BlockSpec((tm, d), lambda i, j: (i, 0)),
                  pl.BlockSpec((1, d), lambda i, j: (0, 0)),
                  pl.BlockSpec((d, tn), lambda i, j: (0, j))],
        out_specs=pl.BlockSpec((tm, tn), lambda i, j: (i, j)),
        out_shape=jax.ShapeDtypeStruct((m, nc), F32),
        scratch_shapes=[pltpu.VMEM((tm, d), BF16)],
        compiler_params=_cparams(("parallel", "arbitrary")),
        name="in_proj",
    )(x2d, g_ln.reshape(1, d), w_bf)


def _out_kernel(x_ref, ya_ref, yb_ref, p_ref, woa_ref, wob_ref, wg_ref, wp_ref, gf_ref, o_ref):
    h2 = (x_ref[...]
          + jnp.dot(ya_ref[...], woa_ref[...], preferred_element_type=F32)
          + jnp.dot(yb_ref[...], wob_ref[...], preferred_element_type=F32))
    gate = _sigmoid(jnp.dot(h2.astype(BF16), wg_ref[...], preferred_element_type=F32))
    ple = jnp.dot(p_ref[...].astype(BF16), wp_ref[...], preferred_element_type=F32)
    h3 = h2 + gate * ple
    ms = jnp.mean(h3 * h3, axis=-1, keepdims=True)
    o_ref[...] = h3 * lax.rsqrt(ms + RMS_EPS) * gf_ref[...]


def _out_block(x2d, ya, yb, p2d, woa, wob, wg, wp, g_final, tm):
    m, d = x2d.shape
    da, db, dp = ya.shape[1], yb.shape[1], p2d.shape[1]
    row = lambda w: pl.BlockSpec((tm, w), lambda i: (i, 0))
    whole = lambda a: pl.BlockSpec(a.shape, lambda i: (0, 0), pipeline_mode=pl.Buffered(1))
    gf = g_final.reshape(1, d)
    return pl.pallas_call(
        _out_kernel,
        grid=(m // tm,),
        in_specs=[row(d), row(da), row(db), row(dp), whole(woa), whole(wob), whole(wg), whole(wp),
                  whole(gf)],
        out_specs=row(d),
        out_shape=jax.ShapeDtypeStruct((m, d), F32),
        compiler_params=_cparams(("parallel",)),
        name="out_block",
    )(x2d, ya, yb, p2d, woa, wob, wg, wp, gf)


def _rwkv_vectors(xr, xk, xv, xl, w0, a0, k_k, k_a, r_k, ww_pad, wa_pad, seg):
    lw = _bdot(jnp.tanh(xl), ww_pad)
    la = _bdot(xl, wa_pad)
    logw = -DECAY_SCALE * _sigmoid(w0 + lw)
    a = _sigmoid(a0 + la)
    kk = xk * k_k
    kk = kk * jnp.minimum(lax.rsqrt(_segsum(kk * kk, seg)), 1e12)
    k2 = xk * (1.0 + (a - 1.0) * k_a)
    bonus = _segsum(xr * k2 * r_k, seg) * xv
    return xr, k2, xv, logw, kk, a, bonus


def _rwkv_post(y, bonus, ga, gn_g, gn_b, seg):
    inv = 1.0 / HD_A
    mu = _segsum(y, seg) * inv
    yc = y - mu
    var = _segsum(yc * yc, seg) * inv
    hn = yc * lax.rsqrt(var + GN_EPS_A) * gn_g + gn_b
    return (hn + bonus) * _silu(ga)


def _rwkv_prompt_kernel(zr, zk, zv, zg, zl, pr, pk, pv, plr, sr, sk, sv, slr,
                        mur, muk, muv, mul, w0, a0, k_k, k_a, r_k, gng, gnb, ww, wa,
                        y_ref, s_ref,
                        S_scr, r_s, k_s, v_s, lw_s, kk_s, a_s, y_s):
    t = pl.program_id(2)
    nt = pl.num_programs(2)
    tb = zr.shape[0]
    gps = zr.shape[1] // GROUP
    nchunk = tb // CHUNK_A
    seg = _seg_ones(GROUP, HD_A)

    @pl.when(t == 0)
    def _():
        S_scr[...] = jnp.zeros_like(S_scr)

    row = lax.broadcasted_iota(jnp.int32, (tb, 1), 0)

    def shifted(z_ref, p_ref, s0_ref, mu_ref):
        x = z_ref[...]
        last = jnp.where(t == 0, s0_ref[...], p_ref[7:8, :])
        prev = jnp.where(row == 0, last, pltpu.roll(x, 1, 0))
        return x + mu_ref[...] * (prev - x)

    xr = shifted(zr, pr, sr, mur)
    xk = shifted(zk, pk, sk, muk)
    xv = shifted(zv, pv, sv, muv)
    xl = shifted(zl, plr, slr, mul)
    r, k2, v, logw, kk, a, bonus = _rwkv_vectors(
        xr, xk, xv, xl, w0[...], a0[...], k_k[...], k_a[...], r_k[...], ww[...], wa[...], seg)
    r_s[...] = r
    k_s[...] = k2
    v_s[...] = v
    lw_s[...] = logw
    kk_s[...] = kk
    a_s[...] = a

    lane = lax.broadcasted_iota(jnp.int32, (1, GROUP), 1) // HD_A
    masks = [jnp.where(lane == h, 1.0, 0.0) for h in range(HEADS_PER_GROUP)]
    ci = lax.broadcasted_iota(jnp.int32, (CHUNK_A, CHUNK_A), 0)
    cj = lax.broadcasted_iota(jnp.int32, (CHUNK_A, CHUNK_A), 1)
    tri_incl = jnp.where(ci >= cj, 1.0, 0.0).astype(BF16)
    si = lax.broadcasted_iota(jnp.int32, (STACK, STACK), 0)
    sj = lax.broadcasted_iota(jnp.int32, (STACK, STACK), 1)
    same = (si // CHUNK_A) == (sj // CHUNK_A)
    strict = same & ((si % CHUNK_A) > (sj % CHUNK_A))
    incl = same & ((si % CHUNK_A) >= (sj % CHUNK_A))

    def stack(x):
        return jnp.concatenate([x * m for m in masks], axis=0)

    def chunk_group(sl, g):
        ln = slice(g * GROUP, (g + 1) * GROUP)
        rc, kc, vc, lwc, kkc, ac = r_s[sl, ln], k_s[sl, ln], v_s[sl, ln], lw_s[sl, ln], kk_s[sl, ln], a_s[sl, ln]
        cum = _dot_exact_lhs(tri_incl, lwc)
        e_pos = jnp.exp(cum)
        e_neg = jnp.exp(-cum)
        cum_end = cum[CHUNK_A - 1:CHUNK_A, :]
        e_tail = jnp.exp(cum_end - cum)
        beta = kkc * ac
        a_st = stack(-kkc * jnp.exp(cum - lwc))
        r_st = stack(rc * e_pos)
        b_st = stack(beta * e_neg)
        k_st = stack(kc * e_neg)
        v_st = stack(vc)
        bt_st = stack(beta * e_tail)
        kt_st = stack(kc * e_tail)

        S = S_scr[g]
        yield
        n_ab = jnp.where(strict, _bdot_nt(a_st, b_st), 0.0)
        n_ak = jnp.where(strict, _bdot_nt(a_st, k_st), 0.0)
        yield
        u = _bdot_nt(a_st, S) + _bdot(n_ak, v_st)
        npow = n_ab
        steps = CHUNK_A.bit_length() - 1
        for i in range(steps):
            yield
            u = u + _bdot(npow, u)
            if i + 1 < steps:
                npow = _bdot(npow, npow)
        yield
        w_rb = jnp.where(incl, _bdot_nt(r_st, b_st), 0.0)
        w_rk = jnp.where(incl, _bdot_nt(r_st, k_st), 0.0)
        yield
        y_st = _bdot_nt(r_st, S) + _bdot(w_rb, u) + _bdot(w_rk, v_st)
        y = y_st[0:CHUNK_A]
        for h in range(1, HEADS_PER_GROUP):
            y = y + y_st[h * CHUNK_A:(h + 1) * CHUNK_A]
        y_s[sl, ln] = y
        yield
        S_scr[g] = S * jnp.exp(cum_end) + _bdot_tn(u, bt_st) + _bdot_tn(v_st, kt_st)

    def chunk(j, carry):
        sl = pl.ds(pl.multiple_of(j * CHUNK_A, CHUNK_A), CHUNK_A)
        active = [chunk_group(sl, g) for g in range(gps)]
        while active:
            active = [gen for gen in active if next(gen, True) is None]
        return carry

    lax.fori_loop(0, nchunk, chunk, 0)

    y_ref[...] = _rwkv_post(y_s[...], bonus, zg[...], gng[...], gnb[...], seg).astype(y_ref.dtype)

    @pl.when(t == nt - 1)
    def _():
        s_ref[...] = S_scr[...]


def _rwkv_prompt(z2, sh_rkv, sh_l, mu_rkv, mu_l, w0, a0, k_k, k_a, r_k, gng, gnb, ww_pad, wa_pad, tb, gps):
    b, t, nc = z2.shape
    da = w0.shape[1]
    wd = gps * GROUP
    ng = da // wd
    nt = t // tb
    lora_blk = (nc - LANES) // LANES
    per8 = tb // 8

    def zspec(off):
        return pl.BlockSpec((None, tb, wd), lambda bi, g, ti, off=off: (bi, ti, off + g))

    def pspec(off):
        return pl.BlockSpec((None, 8, wd),
                            lambda bi, g, ti, off=off: (bi, jnp.maximum(ti * per8 - 1, 0), off + g))

    def sspec(off):
        return pl.BlockSpec((None, 1, wd), lambda bi, g, ti, off=off: (bi, 0, off + g))

    def vspec(off=0):
        return pl.BlockSpec((1, wd), lambda bi, g, ti, off=off: (0, off + g))

    zl_spec = pl.BlockSpec((None, tb, LANES), lambda bi, g, ti: (bi, ti, lora_blk))
    pl_spec = pl.BlockSpec((None, 8, LANES),
                           lambda bi, g, ti: (bi, jnp.maximum(ti * per8 - 1, 0), lora_blk))
    sl_spec = pl.BlockSpec((None, 1, LANES), lambda bi, g, ti: (bi, 0, 0))
    ml_spec = pl.BlockSpec((1, LANES), lambda bi, g, ti: (0, 0))
    w_spec = pl.BlockSpec((LANES, wd), lambda bi, g, ti: (0, g))

    in_specs = [zspec(0), zspec(ng), zspec(2 * ng), zspec(3 * ng), zl_spec,
                pspec(0), pspec(ng), pspec(2 * ng), pl_spec,
                sspec(0), sspec(ng), sspec(2 * ng), sl_spec,
                vspec(0), vspec(ng), vspec(2 * ng), ml_spec,
                vspec(), vspec(), vspec(), vspec(), vspec(), vspec(), vspec(), w_spec, w_spec]
    blk = lambda: pltpu.VMEM((tb, wd), F32)
    return pl.pallas_call(
        _rwkv_prompt_kernel,
        grid=(b, ng, nt),
        in_specs=in_specs,
        out_specs=[pl.BlockSpec((None, tb, wd), lambda bi, g, ti: (bi, ti, g)),
                   pl.BlockSpec((None, gps, GROUP, GROUP), lambda bi, g, ti: (bi, g, 0, 0))],
        out_shape=[jax.ShapeDtypeStruct((b, t, da), BF16),
                   jax.ShapeDtypeStruct((b, ng * gps, GROUP, GROUP), F32)],
        scratch_shapes=[pltpu.VMEM((gps, GROUP, GROUP), F32)] + [blk() for _ in range(7)],
        compiler_params=_cparams(("parallel", "parallel", "arbitrary")),
        name="rwkv_prompt",
    )(z2, z2, z2, z2, z2, z2, z2, z2, z2, sh_rkv, sh_rkv, sh_rkv, sh_l,
      mu_rkv, mu_rkv, mu_rkv, mu_l, w0, a0, k_k, k_a, r_k, gng, gnb, ww_pad, wa_pad)


def _rwkv_dec_pre_kernel(zr, zk, zv, zg, zl, pr, pk, pv, plr, mur, muk, muv, mul,
                         w0, a0, k_k, k_a, r_k, ww, wa,
                         r_o, w_o, k_o, v_o, al_o, be_o, bo_o, sg_o):
    seg = _seg_ones(zr.shape[1], HD_A)
    sh = lambda z, p, mu: z[...] + mu[...] * (p[...] - z[...])
    xr, xk, xv, xl = sh(zr, pr, mur), sh(zk, pk, muk), sh(zv, pv, muv), sh(zl, plr, mul)
    r, k2, v, logw, kk, a, bonus = _rwkv_vectors(
        xr, xk, xv, xl, w0[...], a0[...], k_k[...], k_a[...], r_k[...], ww[...], wa[...], seg)
    r_o[...] = r
    w_o[...] = jnp.exp(logw)
    k_o[...] = k2
    v_o[...] = v
    al_o[...] = -kk
    be_o[...] = kk * a
    bo_o[...] = bonus
    sg_o[...] = _silu(zg[...])


def _rwkv_dec_pre(z2, prev_rkv, prev_l, mu_rkv, mu_l, w0, a0, k_k, k_a, r_k, ww_pad, wa_pad):
    b, nc = z2.shape
    da = w0.shape[1]
    nb = da // LANES
    c = lambda w, j: pl.BlockSpec((b, w), lambda i, j=j: (0, j))
    in_specs = [c(da, 0), c(da, 1), c(da, 2), c(da, 3), c(LANES, (nc - LANES) // LANES),
                c(da, 0), c(da, 1), c(da, 2), c(LANES, 0),
                pl.BlockSpec((1, da), lambda i: (0, 0)), pl.BlockSpec((1, da), lambda i: (0, 1)),
                pl.BlockSpec((1, da), lambda i: (0, 2)), pl.BlockSpec((1, LANES), lambda i: (0, 0))]
    in_specs += [pl.BlockSpec((1, da), lambda i: (0, 0))] * 5
    in_specs += [pl.BlockSpec((LANES, da), lambda i: (0, 0))] * 2
    del nb
    return pl.pallas_call(
        _rwkv_dec_pre_kernel,
        grid=(1,),
        in_specs=in_specs,
        out_specs=[pl.BlockSpec((b, da), lambda i: (0, 0))] * 8,
        out_shape=[jax.ShapeDtypeStruct((b, da), F32)] * 8,
        compiler_params=_cparams(("arbitrary",)),
        name="rwkv_dec_pre",
    )(z2, z2, z2, z2, z2, prev_rkv, prev_rkv, prev_rkv, prev_l, mu_rkv, mu_rkv, mu_rkv, mu_l,
      w0, a0, k_k, k_a, r_k, ww_pad, wa_pad)


def _rwkv_dec_state_kernel(s_ref, w_ref, al_ref, be_ref, k_ref, r_ref, v_ref, so_ref, y_ref):
    S = s_ref[...]
    sa = jnp.sum(S * al_ref[...], axis=-1, keepdims=True)
    sn = S * w_ref[...] + sa * be_ref[...] + v_ref[...] * k_ref[...]
    so_ref[...] = sn
    y_ref[...] = jnp.sum(sn * r_ref[...], axis=-1, keepdims=True)


def _rwkv_dec_state(s0, w, al, be, k2, r, v, bb):
    b, h, n, _ = s0.shape
    rows = lambda x: x.reshape(b, h, 1, n)
    sspec = pl.BlockSpec((bb, h, n, n), lambda i: (i, 0, 0, 0))
    rspec = pl.BlockSpec((bb, h, 1, n), lambda i: (i, 0, 0, 0))
    cspec = pl.BlockSpec((bb, h, n, 1), lambda i: (i, 0, 0, 0))
    return pl.pallas_call(
        _rwkv_dec_state_kernel,
        grid=(b // bb,),
        in_specs=[sspec, rspec, rspec, rspec, rspec, rspec, cspec],
        out_specs=[sspec, cspec],
        out_shape=[jax.ShapeDtypeStruct(s0.shape, F32), jax.ShapeDtypeStruct((b, h, n, 1), F32)],
        compiler_params=_cparams(("parallel",)),
        name="rwkv_dec_state",
    )(s0, rows(w), rows(al), rows(be), rows(k2), rows(r), v.reshape(b, h, n, 1))


def _rotate_pairs(x, cos, sin_signed):
    n = x.shape[-1]
    lane = lax.broadcasted_iota(jnp.int32, x.shape, x.ndim - 1)
    nxt = pltpu.roll(x, n - 1, x.ndim - 1)
    prv = pltpu.roll(x, 1, x.ndim - 1)
    swapped = jnp.where(lane % 2 == 0, nxt, prv)
    return x * cos + swapped * sin_signed


def _head_norm_b(y, g, b):
    mu = jnp.mean(y, axis=-1, keepdims=True)
    yc = y - mu
    var = jnp.mean(yc * yc, axis=-1, keepdims=True)
    return yc * lax.rsqrt(var + GN_EPS_B) * g + b


def _ret_prompt_kernel(lg_ref, q_ref, k_ref, v_ref, g_ref, cos_ref, sin_ref, gng, gnb,
                       y_ref, s_ref, S_scr):
    c = pl.program_id(2)
    nc = pl.num_programs(2)
    L, d = q_ref.shape

    @pl.when(c == 0)
    def _():
        S_scr[...] = jnp.zeros_like(S_scr)

    lg = lg_ref[0:1, 0:1]
    cos, sin = cos_ref[...], sin_ref[...]
    q = _rotate_pairs(q_ref[...], cos, sin)
    k = _rotate_pairs(k_ref[...], cos, sin) * (d ** -0.5)
    v = v_ref[...]
    ii = lax.broadcasted_iota(jnp.int32, (L, L), 0)
    jj = lax.broadcasted_iota(jnp.int32, (L, L), 1)
    diff = (ii - jj).astype(F32)
    dmask = jnp.where(diff >= 0, jnp.exp(jnp.maximum(diff, 0.0) * lg), 0.0)
    idx = lax.broadcasted_iota(jnp.int32, (L, 1), 0).astype(F32)
    S = S_scr[...]
    scores = _bdot_nt(q, k) * dmask
    inner = _bdot(scores, v)
    cross = _bdot(q, S) * jnp.exp((idx + 1.0) * lg)
    kdec = jnp.exp((L - 1.0 - idx) * lg)
    S_new = S * jnp.exp(L * lg) + _bdot_tn(k * kdec, v)
    S_scr[...] = S_new
    y = _head_norm_b(inner + cross, gng[...], gnb[...]) * _silu(g_ref[...])
    y_ref[...] = y.astype(y_ref.dtype)

    @pl.when(c == nc - 1)
    def _():
        s_ref[...] = S_new


def _ret_prompt(z2, lg_tile, cos, sin, gng, gnb, col0):
    b, t, _ = z2.shape
    d = cos.shape[1]
    nh = gng.shape[1] // d
    L = CHUNK_B if t % CHUNK_B == 0 else t
    nchunk = t // L
    c0 = col0 // d
    zspec = lambda off: pl.BlockSpec((None, L, d), lambda bi, h, ci, off=off: (bi, ci, c0 + off + h))
    tspec = pl.BlockSpec((L, d), lambda bi, h, ci: (ci, 0))
    vspec = pl.BlockSpec((1, d), lambda bi, h, ci: (0, h))
    return pl.pallas_call(
        _ret_prompt_kernel,
        grid=(b, nh, nchunk),
        in_specs=[pl.BlockSpec((None, 8, LANES), lambda bi, h, ci: (h, 0, 0)),
                  zspec(0), zspec(nh), zspec(2 * nh), zspec(3 * nh), tspec, tspec, vspec, vspec],
        out_specs=[pl.BlockSpec((None, L, d), lambda bi, h, ci: (bi, ci, h)),
                   pl.BlockSpec((None, None, d, d), lambda bi, h, ci: (bi, h, 0, 0))],
        out_shape=[jax.ShapeDtypeStruct((b, t, nh * d), BF16),
                   jax.ShapeDtypeStruct((b, nh, d, d), F32)],
        scratch_shapes=[pltpu.VMEM((d, d), F32)],
        compiler_params=_cparams(("parallel", "parallel", "arbitrary")),
        name="ret_prompt",
    )(lg_tile, z2, z2, z2, z2, cos, sin, gng, gnb)


def _ret_dec_pre_kernel(zq, zk, zv, zg, cos_ref, sin_ref, q_o, k_o, in_o, sg_o):
    d = cos_ref.shape[1] // H_B
    seg = _seg_ones(zq.shape[1], d)
    cos, sin = cos_ref[...], sin_ref[...]
    q = _rotate_pairs(zq[...], cos, sin)
    k = _rotate_pairs(zk[...], cos, sin) * (d ** -0.5)
    q_o[...] = q
    k_o[...] = k
    in_o[...] = _dot_exact_rhs(q * k, seg) * zv[...]
    sg_o[...] = _silu(zg[...])


def _ret_dec_pre(z2, cos, sin, col0):
    b, _ = z2.shape
    db = cos.shape[1]
    c0 = col0 // db
    c = lambda j: pl.BlockSpec((b, db), lambda i, j=j: (0, c0 + j))
    t = pl.BlockSpec((1, db), lambda i: (0, 0))
    return pl.pallas_call(
        _ret_dec_pre_kernel,
        grid=(1,),
        in_specs=[c(0), c(1), c(2), c(3), t, t],
        out_specs=[pl.BlockSpec((b, db), lambda i: (0, 0))] * 4,
        out_shape=[jax.ShapeDtypeStruct((b, db), F32)] * 4,
        compiler_params=_cparams(("arbitrary",)),
        name="ret_dec_pre",
    )(z2, z2, z2, z2, cos, sin)


def _ret_dec_state_kernel(lg_ref, s_ref, q_ref, k_ref, v_ref, so_ref, cr_ref):
    nh = s_ref.shape[1]
    for h in range(nh):
        lg = lg_ref[h, 0:1, 0:1]
        gamma = jnp.exp(lg)
        S = s_ref[:, h]
        cr_ref[:, h] = jnp.sum(S * q_ref[:, h], axis=1, keepdims=True) * gamma
        so_ref[:, h] = S * gamma + k_ref[:, h] * v_ref[:, h]


def _ret_dec_state(lg_tile, s0, q, k, v, bb):
    b, nh, d, _ = s0.shape
    sspec = pl.BlockSpec((bb, nh, d, d), lambda i: (i, 0, 0, 0))
    cspec = pl.BlockSpec((bb, nh, d, 1), lambda i: (i, 0, 0, 0))
    rspec = pl.BlockSpec((bb, nh, 1, d), lambda i: (i, 0, 0, 0))
    return pl.pallas_call(
        _ret_dec_state_kernel,
        grid=(b // bb,),
        in_specs=[pl.BlockSpec(lg_tile.shape, lambda i: (0, 0, 0)), sspec, cspec, cspec, rspec],
        out_specs=[sspec, rspec],
        out_shape=[jax.ShapeDtypeStruct(s0.shape, F32), jax.ShapeDtypeStruct((b, nh, 1, d), F32)],
        compiler_params=_cparams(("parallel",)),
        name="ret_dec_state",
    )(lg_tile, s0, q.reshape(b, nh, d, 1), k.reshape(b, nh, d, 1), v.reshape(b, nh, 1, d))


def _dec_post_kernel(ya_ref, bo_ref, sga_ref, gag, gab, in_ref, cr_ref, sgb_ref, gbg, gbb,
                     oa_ref, ob_ref):
    da = ya_ref.shape[1]
    db = in_ref.shape[1]
    d = db // H_B
    sega = _seg_ones(da, HD_A)
    segb = _seg_ones(db, d)
    y = ya_ref[...]
    mu = _dot_exact_rhs(y, sega) * (1.0 / HD_A)
    yc = y - mu
    var = _dot_exact_rhs(yc * yc, sega) * (1.0 / HD_A)
    hn = yc * lax.rsqrt(var + GN_EPS_A) * gag[...] + gab[...]
    oa_ref[...] = ((hn + bo_ref[...]) * sga_ref[...]).astype(oa_ref.dtype)
    y = in_ref[...] + cr_ref[...]
    mu = _dot_exact_rhs(y, segb) * (1.0 / d)
    yc = y - mu
    var = _dot_exact_rhs(yc * yc, segb) * (1.0 / d)
    hn = yc * lax.rsqrt(var + GN_EPS_B) * gbg[...] + gbb[...]
    ob_ref[...] = (hn * sgb_ref[...]).astype(ob_ref.dtype)


def _dec_post(ya, bonus, sga, gag, gab, inner, cross, sgb, gbg, gbb):
    b, da = ya.shape
    db = inner.shape[1]
    fa = pl.BlockSpec((b, da), lambda i: (0, 0))
    fb = pl.BlockSpec((b, db), lambda i: (0, 0))
    va = pl.BlockSpec((1, da), lambda i: (0, 0))
    vb = pl.BlockSpec((1, db), lambda i: (0, 0))
    return pl.pallas_call(
        _dec_post_kernel,
        grid=(1,),
        in_specs=[fa, fa, fa, va, va, fb, fb, fb, vb, vb],
        out_specs=[fa, fb],
        out_shape=[jax.ShapeDtypeStruct((b, da), BF16), jax.ShapeDtypeStruct((b, db), BF16)],
        compiler_params=_cparams(("arbitrary",)),
        name="dec_post",
    )(ya, bonus, sga, gag, gab, inner, cross, sgb, gbg, gbb)


def _rot_tables(pos, d):
    angle = 1.0 / (ROPE_BASE ** jnp.linspace(0.0, 1.0, d // 2, dtype=F32))
    theta = pos[:, None] * angle[None, :]
    cos = jnp.repeat(jnp.cos(theta), 2, axis=-1)
    sin = jnp.repeat(jnp.sin(theta), 2, axis=-1)
    sign = jnp.tile(jnp.array([-1.0, 1.0], F32), d // 2)
    return cos, sin * sign


def _diag_blocks(s_bd):
    b, g = s_bd.shape[:2]
    s6 = s_bd.reshape(b, g, HEADS_PER_GROUP, HD_A, HEADS_PER_GROUP, HD_A)
    blocks = jnp.stack([s6[:, :, i, :, i, :] for i in range(HEADS_PER_GROUP)], axis=2)
    return blocks.reshape(b, g * HEADS_PER_GROUP, HD_A, HD_A)


def _pick_tile(n, pref):
    for t in pref:
        if n % t == 0:
            return t
    return n


def kernel(x_prompt, x_sample, p_prompt, p_sample, state_wkv, state_shift, state_ret, g_ln, w_in,
           mu_shift, w0, w_wB, a0, w_aB, k_k, k_a, r_k, gn_a_g, gn_a_b, gn_b_g, gn_b_b, w_out, w_ple,
           w_ple_gate, g_final):
    depth = g_ln.shape[0]
    assert depth == 1
    bp, tp, d_model = x_prompt.shape
    bs, ts, _ = x_sample.shape
    assert ts == 1
    da = w0.shape[1]
    db = gn_b_g.shape[1]
    hd_b = db // H_B
    n_rkv = 3 * da
    a_shift = n_rkv + 2 * LORA
    assert LORA * 2 == LANES and a_shift == mu_shift.shape[1]
    assert tp % CHUNK_A == 0 and da % GROUP == 0

    w = w_in[0]
    w_perm = jnp.concatenate([w[:, :n_rkv], w[:, a_shift:], w[:, n_rkv:a_shift]], axis=1).astype(BF16)
    col_b = n_rkv + da
    mu = mu_shift[0]
    mu_rkv = mu[:n_rkv].reshape(1, n_rkv)
    mu_l = mu[n_rkv:].reshape(1, LANES)
    zpad = jnp.zeros((LORA, da), F32)
    ww_pad = jnp.concatenate([w_wB[0], zpad], axis=0).astype(BF16)
    wa_pad = jnp.concatenate([zpad, w_aB[0]], axis=0).astype(BF16)
    v1 = lambda a: a[0].reshape(1, -1)
    w0_, a0_, kk_, ka_, rk_ = v1(w0), v1(a0), v1(k_k), v1(k_a), v1(r_k)
    gag, gab, gbg, gbb = v1(gn_a_g), v1(gn_a_b), v1(gn_b_g), v1(gn_b_b)
    woa = w_out[0][:da].astype(BF16)
    wob = w_out[0][da:].astype(BF16)
    wg = w_ple_gate[0].astype(BF16)
    wp = w_ple[0].astype(BF16)
    lg = jnp.log(1.0 - jnp.exp2(-5.0 - jnp.arange(H_B, dtype=F32)))
    lg_tile = jnp.broadcast_to(lg[:, None, None], (H_B, 8, LANES))

    tn = _pick_tile(w_perm.shape[1], (640, 128))

    xp = x_prompt.reshape(bp * tp, d_model)
    zp = _proj(xp, g_ln[0], w_perm, _pick_tile(bp * tp, (1024, 512, 256, 128)), tn)
    zp3 = zp.reshape(bp, tp, -1)
    sh0 = jnp.zeros((bp, 1, a_shift), F32)
    tb = _pick_tile(tp, (256, 128, 64))
    ya_p, s_bd = _rwkv_prompt(zp3, sh0[:, :, :n_rkv], sh0[:, :, n_rkv:], mu_rkv, mu_l, w0_, a0_, kk_, ka_,
                              rk_, gag, gab, ww_pad, wa_pad, tb, da // GROUP)
    cos_p, sin_p = _rot_tables(jnp.arange(tp, dtype=F32), hd_b)
    yb_p, ret_p = _ret_prompt(zp3, lg_tile, cos_p, sin_p, gbg, gbb, col_b)
    y_prompt = _out_block(xp, ya_p.reshape(bp * tp, da), yb_p.reshape(bp * tp, db),
                          p_prompt[0].reshape(bp * tp, -1), woa, wob, wg, wp, g_final,
                          _pick_tile(bp * tp, (256, 128))).reshape(bp, tp, d_model)
    wkv_p = _diag_blocks(s_bd)
    shift_p = jnp.concatenate([zp3[:, -1, :n_rkv], zp3[:, -1, -LANES:]], axis=-1)

    xs = x_sample.reshape(bs, d_model)
    zs = _proj(xs, g_ln[0], w_perm, bs, tn)
    prev = state_shift[0]
    r, wdec, k2, v, al, be, bonus, sga = _rwkv_dec_pre(
        zs, prev[:, :n_rkv], prev[:, n_rkv:], mu_rkv, mu_l, w0_, a0_, kk_, ka_, rk_, ww_pad, wa_pad)
    h_a = da // HD_A
    hs = lambda a: a.reshape(bs, h_a, HD_A)
    wkv_s, y_col = _rwkv_dec_state(state_wkv[0], hs(wdec), hs(al), hs(be), hs(k2), hs(r), hs(v),
                                   _pick_tile(bs, (4, 2, 1)))
    cos_s, sin_s = _rot_tables((PAST_LEN + jnp.arange(ts)).astype(F32), hd_b)
    cos_s = jnp.tile(cos_s, (1, H_B))
    sin_s = jnp.tile(sin_s, (1, H_B))
    q_s, k_s, inner, sgb = _ret_dec_pre(zs, cos_s, sin_s, col_b)
    hb = lambda a: a.reshape(bs, H_B, hd_b)
    ret_s, cross = _ret_dec_state(lg_tile, state_ret[0], hb(q_s), hb(k_s),
                                  hb(zs[:, col_b + 2 * db:col_b + 3 * db]), _pick_tile(bs, (4, 2, 1)))
    ya_s, yb_s = _dec_post(y_col.reshape(bs, da), bonus, sga, gag, gab, inner, cross.reshape(bs, db),
                           sgb, gbg, gbb)
    y_sample = _out_block(xs, ya_s, yb_s, p_sample[0].reshape(bs, -1), woa, wob, wg, wp, g_final,
                          bs).reshape(bs, 1, d_model)
    shift_s = jnp.concatenate([zs[:, :n_rkv], zs[:, -LANES:]], axis=-1)

    return (y_prompt, y_sample, wkv_p[None], shift_p[None], ret_p[None],
            wkv_s[None], shift_s[None], ret_s[None])
```

```python
import functools

import jax
import jax.numpy as jnp
from jax import lax
from jax.experimental import pallas as pl
from jax.experimental.pallas import tpu as pltpu

F32 = jnp.float32
BF16 = jnp.bfloat16

HD_A = 64
LORA = 64
H_B = 4
ROPE_BASE = 10000.0
RMS_EPS = 1e-6
GN_EPS_A = HD_A * 1e-5
GN_EPS_B = 1e-5
PAST_LEN = 16384
DECAY_SCALE = 0.6065306597126334

LANES = 128
GROUP = 256
HEADS_PER_GROUP = GROUP // HD_A
CHUNK_A = 64
STACK = HEADS_PER_GROUP * CHUNK_A
CHUNK_B = 256
VMEM_LIMIT = 48 * 1024 * 1024


def _cparams(sem):
    return pltpu.CompilerParams(dimension_semantics=sem, vmem_limit_bytes=VMEM_LIMIT)


def _bdot(a, b):
    return jnp.dot(a.astype(BF16), b.astype(BF16), preferred_element_type=F32)


def _bdot_nt(a, b):
    return lax.dot_general(a.astype(BF16), b.astype(BF16), (((1,), (1,)), ((), ())),
                           preferred_element_type=F32)


def _bdot_tn(a, b):
    return lax.dot_general(a.astype(BF16), b.astype(BF16), (((0,), (0,)), ((), ())),
                           preferred_element_type=F32)


def _split3(x):
    hi = x.astype(BF16)
    r1 = x - hi.astype(F32)
    mid = r1.astype(BF16)
    lo = (r1 - mid.astype(F32)).astype(BF16)
    return hi, mid, lo


def _dot_exact_rhs(x, m01):
    hi = x.astype(BF16)
    lo = (x - hi.astype(F32)).astype(BF16)
    d = lambda t: jnp.dot(t, m01, preferred_element_type=F32)
    return d(hi) + d(lo)


def _dot_exact_lhs(m01, x):
    hi, mid, lo = _split3(x)
    d = lambda t: jnp.dot(m01, t, preferred_element_type=F32)
    return d(hi) + d(mid) + d(lo)


def _segsum(x, seg01):
    n = seg01.shape[0]
    parts = [_dot_exact_rhs(x[:, i:i + n], seg01) for i in range(0, x.shape[1], n)]
    return parts[0] if len(parts) == 1 else jnp.concatenate(parts, axis=1)


def _seg_ones(n, seg):
    i = lax.broadcasted_iota(jnp.int32, (n, n), 0) // seg
    j = lax.broadcasted_iota(jnp.int32, (n, n), 1) // seg
    return jnp.where(i == j, 1.0, 0.0).astype(BF16)


def _sigmoid(x):
    return 1.0 / (1.0 + jnp.exp(-x))


def _silu(x):
    return x * _sigmoid(x)


def _proj_kernel(x_ref, g_ref, w_ref, wl_ref, o_ref, ol_ref, u_ref):
    @pl.when(pl.program_id(1) == 0)
    def _():
        x = x_ref[...]
        ms = jnp.mean(x * x, axis=-1, keepdims=True)
        u = (x * lax.rsqrt(ms + RMS_EPS) * g_ref[...]).astype(BF16)
        u_ref[...] = u
        ol_ref[...] = jnp.dot(u, wl_ref[...], preferred_element_type=F32)

    o_ref[...] = jnp.dot(u_ref[...], w_ref[...], preferred_element_type=F32)


def _proj(x2d, g_ln, w_main, w_lora, tm, tn):
    m, d = x2d.shape
    nc = w_main.shape[1]
    nl = w_lora.shape[1]
    return pl.pallas_call(
        _proj_kernel,
        grid=(m // tm, nc // tn),
        in_specs=[pl.BlockSpec((tm, d), lambda i, j: (i, 0)),
                  pl.BlockSpec((1, d), lambda i, j: (0, 0)),
                  pl.BlockSpec((d, tn), lambda i, j: (0, j)),
                  pl.BlockSpec((d, nl), lambda i, j: (0, 0))],
        out_specs=[pl.BlockSpec((tm, tn), lambda i, j: (i, j)),
                   pl.BlockSpec((tm, nl), lambda i, j: (i, 0))],
        out_shape=[jax.ShapeDtypeStruct((m, nc), F32), jax.ShapeDtypeStruct((m, nl), F32)],
        scratch_shapes=[pltpu.VMEM((tm, d), BF16)],
        compiler_params=_cparams(("parallel", "arbitrary")),
        name="in_proj",
    )(x2d, g_ln.reshape(1, d), w_main, w_lora)


def _out_kernel(x_ref, ya_ref, yb_ref, p_ref, woa_ref, wob_ref, wg_ref, wp_ref, gf_ref, o_ref):
    h2 = (x_ref[...]
          + jnp.dot(ya_ref[...], woa_ref[...], preferred_element_type=F32)
          + jnp.dot(yb_ref[...], wob_ref[...], preferred_element_type=F32))
    gate = _sigmoid(jnp.dot(h2.astype(BF16), wg_ref[...], preferred_element_type=F32))
    ple = jnp.dot(p_ref[...].astype(BF16), wp_ref[...], preferred_element_type=F32)
    h3 = h2 + gate * ple
    ms = jnp.mean(h3 * h3, axis=-1, keepdims=True)
    o_ref[...] = h3 * lax.rsqrt(ms + RMS_EPS) * gf_ref[...]


def _out_block(x2d, ya, yb, p2d, woa, wob, wg, wp, g_final, tm):
    m, d = x2d.shape
    da, db, dp = ya.shape[1], yb.shape[1], p2d.shape[1]
    row = lambda w: pl.BlockSpec((tm, w), lambda i: (i, 0))
    whole = lambda a: pl.BlockSpec(a.shape, lambda i: (0, 0), pipeline_mode=pl.Buffered(1))
    gf = g_final.reshape(1, d)
    return pl.pallas_call(
        _out_kernel,
        grid=(m // tm,),
        in_specs=[row(d), row(da), row(db), row(dp), whole(woa), whole(wob), whole(wg), whole(wp),
                  whole(gf)],
        out_specs=row(d),
        out_shape=jax.ShapeDtypeStruct((m, d), F32),
        compiler_params=_cparams(("parallel",)),
        name="out_block",
    )(x2d, ya, yb, p2d, woa, wob, wg, wp, gf)


def _rwkv_vectors(xr, xk, xv, xl, w0, a0, k_k, k_a, r_k, ww_pad, wa_pad, seg):
    lw = _bdot(jnp.tanh(xl), ww_pad)
    la = _bdot(xl, wa_pad)
    logw = -DECAY_SCALE * _sigmoid(w0 + lw)
    a = _sigmoid(a0 + la)
    kk = xk * k_k
    kk = kk * jnp.minimum(lax.rsqrt(_segsum(kk * kk, seg)), 1e12)
    k2 = xk * (1.0 + (a - 1.0) * k_a)
    bonus = _segsum(xr * k2 * r_k, seg) * xv
    return xr, k2, xv, logw, kk, a, bonus


def _rwkv_post(y, bonus, ga, gn_g, gn_b, seg):
    inv = 1.0 / HD_A
    mu = _segsum(y, seg) * inv
    yc = y - mu
    var = _segsum(yc * yc, seg) * inv
    hn = yc * lax.rsqrt(var + GN_EPS_A) * gn_g + gn_b
    return (hn + bonus) * _silu(ga)


def _rwkv_prompt_kernel(zr, zk, zv, zg, zl, pr, pk, pv, plr, sr, sk, sv, slr,
                        mur, muk, muv, mul, w0, a0, k_k, k_a, r_k, gng, gnb, ww, wa,
                        y_ref, s_ref,
                        S_scr, r_s, k_s, v_s, lw_s, kk_s, a_s, y_s):
    t = pl.program_id(2)
    nt = pl.num_programs(2)
    tb = zr.shape[0]
    gps = zr.shape[1] // GROUP
    nchunk = tb // CHUNK_A
    seg = _seg_ones(GROUP, HD_A)

    @pl.when(t == 0)
    def _():
        S_scr[...] = jnp.zeros_like(S_scr)

    row = lax.broadcasted_iota(jnp.int32, (tb, 1), 0)

    def shifted(z_ref, p_ref, s0_ref, mu_ref):
        x = z_ref[...]
        last = jnp.where(t == 0, s0_ref[...], p_ref[7:8, :])
        prev = jnp.where(row == 0, last, pltpu.roll(x, 1, 0))
        return x + mu_ref[...] * (prev - x)

    xr = shifted(zr, pr, sr, mur)
    xk = shifted(zk, pk, sk, muk)
    xv = shifted(zv, pv, sv, muv)
    xl = shifted(zl, plr, slr, mul)
    r, k2, v, logw, kk, a, bonus = _rwkv_vectors(
        xr, xk, xv, xl, w0[...], a0[...], k_k[...], k_a[...], r_k[...], ww[...], wa[...], seg)
    r_s[...] = r
    k_s[...] = k2
    v_s[...] = v
    lw_s[...] = logw
    kk_s[...] = kk
    a_s[...] = a

    lane = lax.broadcasted_iota(jnp.int32, (1, GROUP), 1) // HD_A
    masks = [jnp.where(lane == h, 1.0, 0.0) for h in range(HEADS_PER_GROUP)]
    ci = lax.broadcasted_iota(jnp.int32, (CHUNK_A, CHUNK_A), 0)
    cj = lax.broadcasted_iota(jnp.int32, (CHUNK_A, CHUNK_A), 1)
    tri_incl = jnp.where(ci >= cj, 1.0, 0.0).astype(BF16)
    si = lax.broadcasted_iota(jnp.int32, (STACK, STACK), 0)
    sj = lax.broadcasted_iota(jnp.int32, (STACK, STACK), 1)
    same = (si // CHUNK_A) == (sj // CHUNK_A)
    strict = same & ((si % CHUNK_A) > (sj % CHUNK_A))
    incl = same & ((si % CHUNK_A) >= (sj % CHUNK_A))

    def stack(x):
        return jnp.concatenate([x * m for m in masks], axis=0)

    def chunk_group(sl, g):
        ln = slice(g * GROUP, (g + 1) * GROUP)
        rc, kc, vc, lwc, kkc, ac = r_s[sl, ln], k_s[sl, ln], v_s[sl, ln], lw_s[sl, ln], kk_s[sl, ln], a_s[sl, ln]
        cum = _dot_exact_lhs(tri_incl, lwc)
        e_pos = jnp.exp(cum)
        e_neg = jnp.exp(-cum)
        cum_end = cum[CHUNK_A - 1:CHUNK_A, :]
        e_tail = jnp.exp(cum_end - cum)
        beta = kkc * ac
        a_st = stack(-kkc * jnp.exp(cum - lwc))
        r_st = stack(rc * e_pos)
        b_st = stack(beta * e_neg)
        k_st = stack(kc * e_neg)
        v_st = stack(vc)
        bt_st = stack(beta * e_tail)
        kt_st = stack(kc * e_tail)

        S = S_scr[g]
        yield
        n_ab = jnp.where(strict, _bdot_nt(a_st, b_st), 0.0)
        n_ak = jnp.where(strict, _bdot_nt(a_st, k_st), 0.0)
        yield
        u = _bdot_nt(a_st, S) + _bdot(n_ak, v_st)
        npow = n_ab
        steps = CHUNK_A.bit_length() - 1
        for i in range(steps):
            yield
            u = u + _bdot(npow, u)
            if i + 1 < steps:
                npow = _bdot(npow, npow)
        yield
        w_rb = jnp.where(incl, _bdot_nt(r_st, b_st), 0.0)
        w_rk = jnp.where(incl, _bdot_nt(r_st, k_st), 0.0)
        yield
        y_st = _bdot_nt(r_st, S) + _bdot(w_rb, u) + _bdot(w_rk, v_st)
        y = y_st[0:CHUNK_A]
        for h in range(1, HEADS_PER_GROUP):
            y = y + y_st[h * CHUNK_A:(h + 1) * CHUNK_A]
        y_s[sl, ln] = y
        yield
        S_scr[g] = S * jnp.exp(cum_end) + _bdot_tn(u, bt_st) + _bdot_tn(v_st, kt_st)

    def chunk(j, carry):
        sl = pl.ds(pl.multiple_of(j * CHUNK_A, CHUNK_A), CHUNK_A)
        active = [chunk_group(sl, g) for g in range(gps)]
        while active:
            active = [gen for gen in active if next(gen, True) is None]
        return carry

    lax.fori_loop(0, nchunk, chunk, 0)

    y_ref[...] = _rwkv_post(y_s[...], bonus, zg[...], gng[...], gnb[...], seg).astype(y_ref.dtype)

    @pl.when(t == nt - 1)
    def _():
        s_ref[...] = S_scr[...]


def _rwkv_prompt(z2, zl, sh_rkv, sh_l, mu_rkv, mu_l, w0, a0, k_k, k_a, r_k, gng, gnb, ww_pad, wa_pad, tb, gps):
    b, t, _ = z2.shape
    da = w0.shape[1]
    wd = gps * GROUP
    ng = da // wd
    nt = t // tb
    per8 = tb // 8

    def zspec(off):
        return pl.BlockSpec((None, tb, wd), lambda bi, g, ti, off=off: (bi, ti, off + g))

    def pspec(off):
        return pl.BlockSpec((None, 8, wd),
                            lambda bi, g, ti, off=off: (bi, jnp.maximum(ti * per8 - 1, 0), off + g))

    def sspec(off):
        return pl.BlockSpec((None, 1, wd), lambda bi, g, ti, off=off: (bi, 0, off + g))

    def vspec(off=0):
        return pl.BlockSpec((1, wd), lambda bi, g, ti, off=off: (0, off + g))

    zl_spec = pl.BlockSpec((None, tb, LANES), lambda bi, g, ti: (bi, ti, 0))
    pl_spec = pl.BlockSpec((None, 8, LANES),
                           lambda bi, g, ti: (bi, jnp.maximum(ti * per8 - 1, 0), 0))
    sl_spec = pl.BlockSpec((None, 1, LANES), lambda bi, g, ti: (bi, 0, 0))
    ml_spec = pl.BlockSpec((1, LANES), lambda bi, g, ti: (0, 0))
    w_spec = pl.BlockSpec((LANES, wd), lambda bi, g, ti: (0, g))

    in_specs = [zspec(0), zspec(ng), zspec(2 * ng), zspec(3 * ng), zl_spec,
                pspec(0), pspec(ng), pspec(2 * ng), pl_spec,
                sspec(0), sspec(ng), sspec(2 * ng), sl_spec,
                vspec(0), vspec(ng), vspec(2 * ng), ml_spec,
                vspec(), vspec(), vspec(), vspec(), vspec(), vspec(), vspec(), w_spec, w_spec]
    blk = lambda: pltpu.VMEM((tb, wd), F32)
    return pl.pallas_call(
        _rwkv_prompt_kernel,
        grid=(b, ng, nt),
        in_specs=in_specs,
        out_specs=[pl.BlockSpec((None, tb, wd), lambda bi, g, ti: (bi, ti, g)),
                   pl.BlockSpec((None, gps, GROUP, GROUP), lambda bi, g, ti: (bi, g, 0, 0))],
        out_shape=[jax.ShapeDtypeStruct((b, t, da), BF16),
                   jax.ShapeDtypeStruct((b, ng * gps, GROUP, GROUP), F32)],
        scratch_shapes=[pltpu.VMEM((gps, GROUP, GROUP), F32)] + [blk() for _ in range(7)],
        compiler_params=_cparams(("parallel", "parallel", "arbitrary")),
        name="rwkv_prompt",
    )(z2, z2, z2, z2, zl, z2, z2, z2, zl, sh_rkv, sh_rkv, sh_rkv, sh_l,
      mu_rkv, mu_rkv, mu_rkv, mu_l, w0, a0, k_k, k_a, r_k, gng, gnb, ww_pad, wa_pad)


def _rwkv_dec_pre_kernel(zr, zk, zv, zg, zl, pr, pk, pv, plr, mur, muk, muv, mul,
                         w0, a0, k_k, k_a, r_k, ww, wa,
                         r_o, w_o, k_o, v_o, al_o, be_o, bo_o, sg_o):
    seg = _seg_ones(zr.shape[1], HD_A)
    sh = lambda z, p, mu: z[...] + mu[...] * (p[...] - z[...])
    xr, xk, xv, xl = sh(zr, pr, mur), sh(zk, pk, muk), sh(zv, pv, muv), sh(zl, plr, mul)
    r, k2, v, logw, kk, a, bonus = _rwkv_vectors(
        xr, xk, xv, xl, w0[...], a0[...], k_k[...], k_a[...], r_k[...], ww[...], wa[...], seg)
    r_o[...] = r
    w_o[...] = jnp.exp(logw)
    k_o[...] = k2
    v_o[...] = v
    al_o[...] = -kk
    be_o[...] = kk * a
    bo_o[...] = bonus
    sg_o[...] = _silu(zg[...])


def _rwkv_dec_pre(z2, zl, prev_rkv, prev_l, mu_rkv, mu_l, w0, a0, k_k, k_a, r_k, ww_pad, wa_pad):
    b, _ = z2.shape
    da = w0.shape[1]
    c = lambda w, j: pl.BlockSpec((b, w), lambda i, j=j: (0, j))
    in_specs = [c(da, 0), c(da, 1), c(da, 2), c(da, 3), c(LANES, 0),
                c(da, 0), c(da, 1), c(da, 2), c(LANES, 0),
                pl.BlockSpec((1, da), lambda i: (0, 0)), pl.BlockSpec((1, da), lambda i: (0, 1)),
                pl.BlockSpec((1, da), lambda i: (0, 2)), pl.BlockSpec((1, LANES), lambda i: (0, 0))]
    in_specs += [pl.BlockSpec((1, da), lambda i: (0, 0))] * 5
    in_specs += [pl.BlockSpec((LANES, da), lambda i: (0, 0))] * 2
    return pl.pallas_call(
        _rwkv_dec_pre_kernel,
        grid=(1,),
        in_specs=in_specs,
        out_specs=[pl.BlockSpec((b, da), lambda i: (0, 0))] * 8,
        out_shape=[jax.ShapeDtypeStruct((b, da), F32)] * 8,
        compiler_params=_cparams(("arbitrary",)),
        name="rwkv_dec_pre",
    )(z2, z2, z2, z2, zl, prev_rkv, prev_rkv, prev_rkv, prev_l, mu_rkv, mu_rkv, mu_rkv, mu_l,
      w0, a0, k_k, k_a, r_k, ww_pad, wa_pad)


def _rwkv_dec_state_kernel(s_ref, w_ref, al_ref, be_ref, k_ref, r_ref, v_ref, so_ref, y_ref):
    bb, nh, n, _ = s_ref.shape
    ii = lax.broadcasted_iota(jnp.int32, (n, n), 0)
    jj = lax.broadcasted_iota(jnp.int32, (n, n), 1)
    eye = jnp.where(ii == jj, 1.0, 0.0)

    def one_head(b, h):
        row = lambda ref: ref[b, h:h + 1, :]
        S = s_ref[b, h]
        sa = jnp.sum(S * row(al_ref), axis=1, keepdims=True)
        v_col = jnp.sum(eye * row(v_ref), axis=1, keepdims=True)
        yield
        sn = S * row(w_ref) + sa * row(be_ref) + v_col * row(k_ref)
        so_ref[b, h] = sn
        y_col = jnp.sum(sn * row(r_ref), axis=1, keepdims=True)
        yield
        y_ref[b, h:h + 1, :] = jnp.sum(eye * y_col, axis=0, keepdims=True)

    def one_batch(b, carry):
        active = [one_head(b, h) for h in range(nh)]
        while active:
            active = [gen for gen in active if next(gen, True) is None]
        return carry

    lax.fori_loop(0, bb, one_batch, 0)


def _rwkv_dec_state(s0, w, al, be, k2, r, v, bb):
    b, h, n, _ = s0.shape
    sspec = pl.BlockSpec((bb, h, n, n), lambda i: (i, 0, 0, 0))
    rspec = pl.BlockSpec((bb, h, n), lambda i: (i, 0, 0))
    return pl.pallas_call(
        _rwkv_dec_state_kernel,
        grid=(b // bb,),
        in_specs=[sspec] + [rspec] * 6,
        out_specs=[sspec, rspec],
        out_shape=[jax.ShapeDtypeStruct(s0.shape, F32), jax.ShapeDtypeStruct((b, h, n), F32)],
        compiler_params=_cparams(("parallel",)),
        name="rwkv_dec_state",
    )(s0, w, al, be, k2, r, v)


def _rotate_pairs(x, cos, sin_signed):
    n = x.shape[-1]
    lane = lax.broadcasted_iota(jnp.int32, x.shape, x.ndim - 1)
    nxt = pltpu.roll(x, n - 1, x.ndim - 1)
    prv = pltpu.roll(x, 1, x.ndim - 1)
    swapped = jnp.where(lane % 2 == 0, nxt, prv)
    return x * cos + swapped * sin_signed


def _head_norm_b(y, g, b):
    mu = jnp.mean(y, axis=-1, keepdims=True)
    yc = y - mu
    var = jnp.mean(yc * yc, axis=-1, keepdims=True)
    return yc * lax.rsqrt(var + GN_EPS_B) * g + b


def _ret_prompt_kernel(lg_ref, q_ref, k_ref, v_ref, g_ref, cos_ref, sin_ref, gng, gnb,
                       y_ref, s_ref, S_scr):
    c = pl.program_id(1)
    nc = pl.num_programs(1)
    L = q_ref.shape[0]
    d = cos_ref.shape[1]
    nh = q_ref.shape[1] // d

    @pl.when(c == 0)
    def _():
        S_scr[...] = jnp.zeros_like(S_scr)

    cos, sin = cos_ref[...], sin_ref[...]
    ii = lax.broadcasted_iota(jnp.int32, (L, L), 0)
    jj = lax.broadcasted_iota(jnp.int32, (L, L), 1)
    diff = (ii - jj).astype(F32)
    idx = lax.broadcasted_iota(jnp.int32, (L, 1), 0).astype(F32)

    def head(h):
        ln = slice(h * d, (h + 1) * d)
        lg = lg_ref[h, 0:1, 0:1]
        q = _rotate_pairs(q_ref[:, ln], cos, sin)
        k = _rotate_pairs(k_ref[:, ln], cos, sin) * (d ** -0.5)
        v = v_ref[:, ln]
        dmask = jnp.where(diff >= 0, jnp.exp(jnp.maximum(diff, 0.0) * lg), 0.0)
        S = S_scr[h]
        yield
        scores = _bdot_nt(q, k) * dmask
        cross = _bdot(q, S) * jnp.exp((idx + 1.0) * lg)
        yield
        inner = _bdot(scores, v)
        kdec = jnp.exp((L - 1.0 - idx) * lg)
        S_scr[h] = S * jnp.exp(L * lg) + _bdot_tn(k * kdec, v)
        yield
        y = _head_norm_b(inner + cross, gng[:, ln], gnb[:, ln]) * _silu(g_ref[:, ln])
        y_ref[:, ln] = y.astype(y_ref.dtype)

    active = [head(h) for h in range(nh)]
    while active:
        active = [gen for gen in active if next(gen, True) is None]

    @pl.when(c == nc - 1)
    def _():
        s_ref[...] = S_scr[...]


def _ret_prompt(z2, lg_tile, cos, sin, gng, gnb, col0):
    b, t, _ = z2.shape
    d = cos.shape[1]
    db = gng.shape[1]
    nh = db // d
    L = _pick_tile(t, (CHUNK_B,))
    nchunk = t // L
    c0 = col0 // db
    zspec = lambda off: pl.BlockSpec((None, L, db), lambda bi, ci, off=off: (bi, ci, c0 + off))
    tspec = pl.BlockSpec((L, d), lambda bi, ci: (ci, 0))
    vspec = pl.BlockSpec((1, db), lambda bi, ci: (0, 0))
    return pl.pallas_call(
        _ret_prompt_kernel,
        grid=(b, nchunk),
        in_specs=[pl.BlockSpec(lg_tile.shape, lambda bi, ci: (0, 0, 0)),
                  zspec(0), zspec(1), zspec(2), zspec(3), tspec, tspec, vspec, vspec],
        out_specs=[pl.BlockSpec((None, L, db), lambda bi, ci: (bi, ci, 0)),
                   pl.BlockSpec((None, nh, d, d), lambda bi, ci: (bi, 0, 0, 0))],
        out_shape=[jax.ShapeDtypeStruct((b, t, db), BF16),
                   jax.ShapeDtypeStruct((b, nh, d, d), F32)],
        scratch_shapes=[pltpu.VMEM((nh, d, d), F32)],
        compiler_params=_cparams(("parallel", "arbitrary")),
        name="ret_prompt",
    )(lg_tile, z2, z2, z2, z2, cos, sin, gng, gnb)


def _ret_dec_pre_kernel(zq, zk, zv, zg, cos_ref, sin_ref, q_o, k_o, in_o, sg_o):
    d = cos_ref.shape[1] // H_B
    seg = _seg_ones(zq.shape[1], d)
    cos, sin = cos_ref[...], sin_ref[...]
    q = _rotate_pairs(zq[...], cos, sin)
    k = _rotate_pairs(zk[...], cos, sin) * (d ** -0.5)
    q_o[...] = q
    k_o[...] = k
    in_o[...] = _dot_exact_rhs(q * k, seg) * zv[...]
    sg_o[...] = _silu(zg[...])


def _ret_dec_pre(z2, cos, sin, col0):
    b, _ = z2.shape
    db = cos.shape[1]
    c0 = col0 // db
    c = lambda j: pl.BlockSpec((b, db), lambda i, j=j: (0, c0 + j))
    t = pl.BlockSpec((1, db), lambda i: (0, 0))
    return pl.pallas_call(
        _ret_dec_pre_kernel,
        grid=(1,),
        in_specs=[c(0), c(1), c(2), c(3), t, t],
        out_specs=[pl.BlockSpec((b, db), lambda i: (0, 0))] * 4,
        out_shape=[jax.ShapeDtypeStruct((b, db), F32)] * 4,
        compiler_params=_cparams(("arbitrary",)),
        name="ret_dec_pre",
    )(z2, z2, z2, z2, cos, sin)


def _ret_dec_state_kernel(lg_ref, s_ref, q_ref, k_ref, v_ref, so_ref, cr_ref):
    bb, nh, d, _ = s_ref.shape
    rowid = lax.broadcasted_iota(jnp.int32, (bb, 1), 0)
    for h in range(nh):
        ln = slice(h * d, (h + 1) * d)
        gamma = jnp.exp(lg_ref[h, 0:1, 0:1])
        q, k, v = q_ref[:, ln], k_ref[:, ln], v_ref[:, ln]
        cross = jnp.zeros((bb, d), F32)
        for b in range(bb):
            sel = rowid == b
            S = s_ref[b, h]
            cross = cross + _bdot(jnp.where(sel, q, 0.0), S)
            so_ref[b, h] = S * gamma + _bdot_tn(jnp.where(sel, k, 0.0), v)
        cr_ref[:, ln] = cross * gamma


def _ret_dec_state(lg_tile, s0, q, k, v, bb):
    b, nh, d, _ = s0.shape
    sspec = pl.BlockSpec((bb, nh, d, d), lambda i: (i, 0, 0, 0))
    rspec = pl.BlockSpec((bb, nh * d), lambda i: (i, 0))
    return pl.pallas_call(
        _ret_dec_state_kernel,
        grid=(b // bb,),
        in_specs=[pl.BlockSpec(lg_tile.shape, lambda i: (0, 0, 0)), sspec, rspec, rspec, rspec],
        out_specs=[sspec, rspec],
        out_shape=[jax.ShapeDtypeStruct(s0.shape, F32), jax.ShapeDtypeStruct((b, nh * d), F32)],
        compiler_params=_cparams(("parallel",)),
        name="ret_dec_state",
    )(lg_tile, s0, q, k, v)


def _dec_post_kernel(ya_ref, bo_ref, sga_ref, gag, gab, in_ref, cr_ref, sgb_ref, gbg, gbb,
                     oa_ref, ob_ref):
    da = ya_ref.shape[1]
    db = in_ref.shape[1]
    d = db // H_B
    sega = _seg_ones(da, HD_A)
    segb = _seg_ones(db, d)
    y = ya_ref[...]
    mu = _dot_exact_rhs(y, sega) * (1.0 / HD_A)
    yc = y - mu
    var = _dot_exact_rhs(yc * yc, sega) * (1.0 / HD_A)
    hn = yc * lax.rsqrt(var + GN_EPS_A) * gag[...] + gab[...]
    oa_ref[...] = ((hn + bo_ref[...]) * sga_ref[...]).astype(oa_ref.dtype)
    y = in_ref[...] + cr_ref[...]
    mu = _dot_exact_rhs(y, segb) * (1.0 / d)
    yc = y - mu
    var = _dot_exact_rhs(yc * yc, segb) * (1.0 / d)
    hn = yc * lax.rsqrt(var + GN_EPS_B) * gbg[...] + gbb[...]
    ob_ref[...] = (hn * sgb_ref[...]).astype(ob_ref.dtype)


def _dec_post(ya, bonus, sga, gag, gab, inner, cross, sgb, gbg, gbb):
    b, da = ya.shape
    db = inner.shape[1]
    fa = pl.BlockSpec((b, da), lambda i: (0, 0))
    fb = pl.BlockSpec((b, db), lambda i: (0, 0))
    va = pl.BlockSpec((1, da), lambda i: (0, 0))
    vb = pl.BlockSpec((1, db), lambda i: (0, 0))
    return pl.pallas_call(
        _dec_post_kernel,
        grid=(1,),
        in_specs=[fa, fa, fa, va, va, fb, fb, fb, vb, vb],
        out_specs=[fa, fb],
        out_shape=[jax.ShapeDtypeStruct((b, da), BF16), jax.ShapeDtypeStruct((b, db), BF16)],
        compiler_params=_cparams(("arbitrary",)),
        name="dec_post",
    )(ya, bonus, sga, gag, gab, inner, cross, sgb, gbg, gbb)


def _rot_tables(pos, d):
    angle = 1.0 / (ROPE_BASE ** jnp.linspace(0.0, 1.0, d // 2, dtype=F32))
    theta = pos[:, None] * angle[None, :]
    cos = jnp.repeat(jnp.cos(theta), 2, axis=-1)
    sin = jnp.repeat(jnp.sin(theta), 2, axis=-1)
    sign = jnp.tile(jnp.array([-1.0, 1.0], F32), d // 2)
    return cos, sin * sign


def _diag_blocks(s_bd):
    b, g = s_bd.shape[:2]
    s6 = s_bd.reshape(b, g, HEADS_PER_GROUP, HD_A, HEADS_PER_GROUP, HD_A)
    blocks = jnp.stack([s6[:, :, i, :, i, :] for i in range(HEADS_PER_GROUP)], axis=2)
    return blocks.reshape(b, g * HEADS_PER_GROUP, HD_A, HD_A)


def _pick_tile(n, pref):
    for t in pref:
        if n % t == 0:
            return t
    return n


def kernel(x_prompt, x_sample, p_prompt, p_sample, state_wkv, state_shift, state_ret, g_ln, w_in,
           mu_shift, w0, w_wB, a0, w_aB, k_k, k_a, r_k, gn_a_g, gn_a_b, gn_b_g, gn_b_b, w_out, w_ple,
           w_ple_gate, g_final):
    depth = g_ln.shape[0]
    assert depth == 1
    bp, tp, d_model = x_prompt.shape
    bs, ts, _ = x_sample.shape
    assert ts == 1
    da = w0.shape[1]
    db = gn_b_g.shape[1]
    hd_b = db // H_B
    n_rkv = 3 * da
    a_shift = n_rkv + 2 * LORA
    assert LORA * 2 == LANES and a_shift == mu_shift.shape[1]
    assert tp % CHUNK_A == 0 and da % GROUP == 0

    w = w_in[0]
    w_main = jnp.concatenate([w[:, :n_rkv], w[:, a_shift:]], axis=1).astype(BF16)
    w_lora = w[:, n_rkv:a_shift].astype(BF16)
    col_b = n_rkv + da
    mu = mu_shift[0]
    mu_rkv = mu[:n_rkv].reshape(1, n_rkv)
    mu_l = mu[n_rkv:].reshape(1, LANES)
    zpad = jnp.zeros((LORA, da), F32)
    ww_pad = jnp.concatenate([w_wB[0], zpad], axis=0).astype(BF16)
    wa_pad = jnp.concatenate([zpad, w_aB[0]], axis=0).astype(BF16)
    v1 = lambda a: a[0].reshape(1, -1)
    w0_, a0_, kk_, ka_, rk_ = v1(w0), v1(a0), v1(k_k), v1(k_a), v1(r_k)
    gag, gab, gbg, gbb = v1(gn_a_g), v1(gn_a_b), v1(gn_b_g), v1(gn_b_b)
    woa = w_out[0][:da].astype(BF16)
    wob = w_out[0][da:].astype(BF16)
    wg = w_ple_gate[0].astype(BF16)
    wp = w_ple[0].astype(BF16)
    lg = jnp.log(1.0 - jnp.exp2(-5.0 - jnp.arange(H_B, dtype=F32)))
    lg_tile = jnp.broadcast_to(lg[:, None, None], (H_B, 8, LANES))

    tn = _pick_tile(w_main.shape[1], (1024, 512, 256, 128))

    xp = x_prompt.reshape(bp * tp, d_model)
    zp, zlp = _proj(xp, g_ln[0], w_main, w_lora, _pick_tile(bp * tp, (1024, 512, 256, 128)), tn)
    zp3 = zp.reshape(bp, tp, -1)
    zlp3 = zlp.reshape(bp, tp, LANES)
    sh0 = jnp.zeros((bp, 1, a_shift), F32)
    tb = _pick_tile(tp, (256, 128, 64))
    ya_p, s_bd = _rwkv_prompt(zp3, zlp3, sh0[:, :, :n_rkv], sh0[:, :, n_rkv:], mu_rkv, mu_l, w0_, a0_, kk_,
                              ka_, rk_, gag, gab, ww_pad, wa_pad, tb, da // GROUP)
    cos_p, sin_p = _rot_tables(jnp.arange(tp, dtype=F32), hd_b)
    yb_p, ret_p = _ret_prompt(zp3, lg_tile, cos_p, sin_p, gbg, gbb, col_b)
    y_prompt = _out_block(xp, ya_p.reshape(bp * tp, da), yb_p.reshape(bp * tp, db),
                          p_prompt[0].reshape(bp * tp, -1), woa, wob, wg, wp, g_final,
                          _pick_tile(bp * tp, (256, 128))).reshape(bp, tp, d_model)
    wkv_p = _diag_blocks(s_bd)
    shift_p = jnp.concatenate([zp3[:, -1, :n_rkv], zlp3[:, -1]], axis=-1)

    xs = x_sample.reshape(bs, d_model)
    zs, zls = _proj(xs, g_ln[0], w_main, w_lora, bs, tn)
    prev = state_shift[0]
    r, wdec, k2, v, al, be, bonus, sga = _rwkv_dec_pre(
        zs, zls, prev[:, :n_rkv], prev[:, n_rkv:], mu_rkv, mu_l, w0_, a0_, kk_, ka_, rk_, ww_pad, wa_pad)
    h_a = da // HD_A
    hs = lambda a: a.reshape(bs, h_a, HD_A)
    wkv_s, y_col = _rwkv_dec_state(state_wkv[0], hs(wdec), hs(al), hs(be), hs(k2), hs(r), hs(v),
                                   _pick_tile(bs, (4, 2, 1)))
    cos_s, sin_s = _rot_tables((PAST_LEN + jnp.arange(ts)).astype(F32), hd_b)
    cos_s = jnp.tile(cos_s, (1, H_B))
    sin_s = jnp.tile(sin_s, (1, H_B))
    q_s, k_s, inner, sgb = _ret_dec_pre(zs, cos_s, sin_s, col_b)
    ret_s, cross = _ret_dec_state(lg_tile, state_ret[0], q_s, k_s,
                                  zs[:, col_b + 2 * db:col_b + 3 * db], _pick_tile(bs, (8,)))
    ya_s, yb_s = _dec_post(y_col.reshape(bs, da), bonus, sga, gag, gab, inner, cross.reshape(bs, db),
                           sgb, gbg, gbb)
    y_sample = _out_block(xs, ya_s, yb_s, p_sample[0].reshape(bs, -1), woa, wob, wg, wp, g_final,
                          bs).reshape(bs, 1, d_model)
    shift_s = jnp.concatenate([zs[:, :n_rkv], zls], axis=-1)

    return (y_prompt, y_sample, wkv_p[None], shift_p[None], ret_p[None],
            wkv_s[None], shift_s[None], ret_s[None])
```

```python
import functools

import jax
import jax.numpy as jnp
from jax import lax
from jax.experimental import pallas as pl
from jax.experimental.pallas import tpu as pltpu

F32 = jnp.float32
BF16 = jnp.bfloat16

HD_A = 64
LORA = 64
H_B = 4
ROPE_BASE = 10000.0
RMS_EPS = 1e-6
GN_EPS_A = HD_A * 1e-5
GN_EPS_B = 1e-5
PAST_LEN = 16384
DECAY_SCALE = 0.6065306597126334

LANES = 128
GROUP = 256
HEADS_PER_GROUP = GROUP // HD_A
CHUNK_A = 64
STACK = HEADS_PER_GROUP * CHUNK_A
CHUNK_B = 256
VMEM_LIMIT = 48 * 1024 * 1024


def _cparams(sem):
    return pltpu.CompilerParams(dimension_semantics=sem, vmem_limit_bytes=VMEM_LIMIT)


def _bdot(a, b):
    return jnp.dot(a.astype(BF16), b.astype(BF16), preferred_element_type=F32)


def _bdot_nt(a, b):
    return lax.dot_general(a.astype(BF16), b.astype(BF16), (((1,), (1,)), ((), ())),
                           preferred_element_type=F32)


def _bdot_tn(a, b):
    return lax.dot_general(a.astype(BF16), b.astype(BF16), (((0,), (0,)), ((), ())),
                           preferred_element_type=F32)


def _split3(x):
    hi = x.astype(BF16)
    r1 = x - hi.astype(F32)
    mid = r1.astype(BF16)
    lo = (r1 - mid.astype(F32)).astype(BF16)
    return hi, mid, lo


def _dot_exact_rhs(x, m01):
    hi = x.astype(BF16)
    lo = (x - hi.astype(F32)).astype(BF16)
    d = lambda t: jnp.dot(t, m01, preferred_element_type=F32)
    return d(hi) + d(lo)


def _dot_exact_lhs(m01, x):
    hi, mid, lo = _split3(x)
    d = lambda t: jnp.dot(m01, t, preferred_element_type=F32)
    return d(hi) + d(mid) + d(lo)


def _segsum(x, seg01):
    n = seg01.shape[0]
    parts = [_dot_exact_rhs(x[:, i:i + n], seg01) for i in range(0, x.shape[1], n)]
    return parts[0] if len(parts) == 1 else jnp.concatenate(parts, axis=1)


def _seg_ones(n, seg):
    i = lax.broadcasted_iota(jnp.int32, (n, n), 0) // seg
    j = lax.broadcasted_iota(jnp.int32, (n, n), 1) // seg
    return jnp.where(i == j, 1.0, 0.0).astype(BF16)


def _sigmoid(x):
    return 1.0 / (1.0 + jnp.exp(-x))


def _silu(x):
    return x * _sigmoid(x)


def _proj_kernel(x_ref, g_ref, w_ref, wl_ref, o_ref, ol_ref, u_ref):
    @pl.when(pl.program_id(1) == 0)
    def _():
        x = x_ref[...]
        ms = jnp.mean(x * x, axis=-1, keepdims=True)
        u = (x * lax.rsqrt(ms + RMS_EPS) * g_ref[...]).astype(BF16)
        u_ref[...] = u
        ol_ref[...] = jnp.dot(u, wl_ref[...], preferred_element_type=F32)

    o_ref[...] = jnp.dot(u_ref[...], w_ref[...], preferred_element_type=F32)


def _proj(x2d, g_ln, w_main, w_lora, tm, tn):
    m, d = x2d.shape
    nc = w_main.shape[1]
    nl = w_lora.shape[1]
    return pl.pallas_call(
        _proj_kernel,
        grid=(m // tm, nc // tn),
        in_specs=[pl.BlockSpec((tm, d), lambda i, j: (i, 0)),
                  pl.BlockSpec((1, d), lambda i, j: (0, 0)),
                  pl.BlockSpec((d, tn), lambda i, j: (0, j)),
                  pl.BlockSpec((d, nl), lambda i, j: (0, 0))],
        out_specs=[pl.BlockSpec((tm, tn), lambda i, j: (i, j)),
                   pl.BlockSpec((tm, nl), lambda i, j: (i, 0))],
        out_shape=[jax.ShapeDtypeStruct((m, nc), F32), jax.ShapeDtypeStruct((m, nl), F32)],
        scratch_shapes=[pltpu.VMEM((tm, d), BF16)],
        compiler_params=_cparams(("parallel", "arbitrary")),
        name="in_proj",
    )(x2d, g_ln.reshape(1, d), w_main, w_lora)


def _out_kernel(x_ref, ya_ref, yb_ref, p_ref, woa_ref, wob_ref, wg_ref, wp_ref, gf_ref, o_ref):
    h2 = (x_ref[...]
          + jnp.dot(ya_ref[...], woa_ref[...], preferred_element_type=F32)
          + jnp.dot(yb_ref[...], wob_ref[...], preferred_element_type=F32))
    gate = _sigmoid(jnp.dot(h2.astype(BF16), wg_ref[...], preferred_element_type=F32))
    ple = jnp.dot(p_ref[...].astype(BF16), wp_ref[...], preferred_element_type=F32)
    h3 = h2 + gate * ple
    ms = jnp.mean(h3 * h3, axis=-1, keepdims=True)
    o_ref[...] = h3 * lax.rsqrt(ms + RMS_EPS) * gf_ref[...]


def _out_block(x2d, ya, yb, p2d, woa, wob, wg, wp, g_final, tm):
    m, d = x2d.shape
    da, db, dp = ya.shape[1], yb.shape[1], p2d.shape[1]
    row = lambda w: pl.BlockSpec((tm, w), lambda i: (i, 0))
    whole = lambda a: pl.BlockSpec(a.shape, lambda i: (0, 0), pipeline_mode=pl.Buffered(1))
    gf = g_final.reshape(1, d)
    return pl.pallas_call(
        _out_kernel,
        grid=(m // tm,),
        in_specs=[row(d), row(da), row(db), row(dp), whole(woa), whole(wob), whole(wg), whole(wp),
                  whole(gf)],
        out_specs=row(d),
        out_shape=jax.ShapeDtypeStruct((m, d), F32),
        compiler_params=_cparams(("parallel",)),
        name="out_block",
    )(x2d, ya, yb, p2d, woa, wob, wg, wp, gf)


def _rwkv_vectors(xr, xk, xv, xl, w0, a0, k_k, k_a, r_k, ww_pad, wa_pad, seg):
    lw = _bdot(jnp.tanh(xl), ww_pad)
    la = _bdot(xl, wa_pad)
    logw = -DECAY_SCALE * _sigmoid(w0 + lw)
    a = _sigmoid(a0 + la)
    kk = xk * k_k
    kk = kk * jnp.minimum(lax.rsqrt(_segsum(kk * kk, seg)), 1e12)
    k2 = xk * (1.0 + (a - 1.0) * k_a)
    bonus = _segsum(xr * k2 * r_k, seg) * xv
    return xr, k2, xv, logw, kk, a, bonus


def _rwkv_post(y, bonus, ga, gn_g, gn_b, seg):
    inv = 1.0 / HD_A
    mu = _segsum(y, seg) * inv
    yc = y - mu
    var = _segsum(yc * yc, seg) * inv
    hn = yc * lax.rsqrt(var + GN_EPS_A) * gn_g + gn_b
    return (hn + bonus) * _silu(ga)


def _rwkv_prompt_kernel(zr, zk, zv, zg, zl, pr, pk, pv, plr, sr, sk, sv, slr,
                        mur, muk, muv, mul, w0, a0, k_k, k_a, r_k, gng, gnb, ww, wa,
                        y_ref, s_ref,
                        S_scr, r_s, k_s, v_s, lw_s, kk_s, a_s, y_s, cum_s):
    t = pl.program_id(2)
    nt = pl.num_programs(2)
    tb = zr.shape[0]
    gps = zr.shape[1] // GROUP
    nchunk = tb // CHUNK_A
    seg = _seg_ones(GROUP, HD_A)

    @pl.when(t == 0)
    def _():
        S_scr[...] = jnp.zeros_like(S_scr)

    row = lax.broadcasted_iota(jnp.int32, (tb, 1), 0)

    def shifted(z_ref, p_ref, s0_ref, mu_ref):
        x = z_ref[...]
        last = jnp.where(t == 0, s0_ref[...], p_ref[7:8, :])
        prev = jnp.where(row == 0, last, pltpu.roll(x, 1, 0))
        return x + mu_ref[...] * (prev - x)

    xr = shifted(zr, pr, sr, mur)
    xk = shifted(zk, pk, sk, muk)
    xv = shifted(zv, pv, sv, muv)
    xl = shifted(zl, plr, slr, mul)
    r, k2, v, logw, kk, a, bonus = _rwkv_vectors(
        xr, xk, xv, xl, w0[...], a0[...], k_k[...], k_a[...], r_k[...], ww[...], wa[...], seg)
    r_s[...] = r
    k_s[...] = k2
    v_s[...] = v
    lw_s[...] = logw
    kk_s[...] = kk
    a_s[...] = a
    bi = lax.broadcasted_iota(jnp.int32, (tb, tb), 0)
    bj = lax.broadcasted_iota(jnp.int32, (tb, tb), 1)
    tri_blk = jnp.where(((bi // CHUNK_A) == (bj // CHUNK_A)) & (bi >= bj), 1.0, 0.0).astype(BF16)
    cum_s[...] = _dot_exact_lhs(tri_blk, logw)

    lane = lax.broadcasted_iota(jnp.int32, (1, GROUP), 1) // HD_A
    masks = [jnp.where(lane == h, 1.0, 0.0) for h in range(HEADS_PER_GROUP)]
    si =lax.broadcasted_iota(jnp.int32, (STACK, STACK), 0)
    sj = lax.broadcasted_iota(jnp.int32, (STACK, STACK), 1)
    same = (si // CHUNK_A) == (sj // CHUNK_A)
    strict = same & ((si % CHUNK_A) > (sj % CHUNK_A))
    incl = same & ((si % CHUNK_A) >= (sj % CHUNK_A))

    def stack(x):
        return jnp.concatenate([x * m for m in masks], axis=0)

    def chunk_group(sl, g):
        ln = slice(g * GROUP, (g + 1) * GROUP)
        rc, kc, vc, lwc, kkc, ac = r_s[sl, ln], k_s[sl, ln], v_s[sl, ln], lw_s[sl, ln], kk_s[sl, ln], a_s[sl, ln]
        cum = cum_s[sl, ln]
        e_pos = jnp.exp(cum)
        e_neg = jnp.exp(-cum)
        cum_end = cum[CHUNK_A - 1:CHUNK_A, :]
        e_tail = jnp.exp(cum_end - cum)
        beta = kkc * ac
        a_st = stack(-kkc * jnp.exp(cum - lwc))
        r_st = stack(rc * e_pos)
        b_st = stack(beta * e_neg)
        k_st = stack(kc * e_neg)
        v_st = stack(vc)
        bt_st = stack(beta * e_tail)
        kt_st = stack(kc * e_tail)

        S = S_scr[g]
        yield
        n_ab = jnp.where(strict, _bdot_nt(a_st, b_st), 0.0)
        n_ak = jnp.where(strict, _bdot_nt(a_st, k_st), 0.0)
        yield
        u = _bdot_nt(a_st, S) + _bdot(n_ak, v_st)
        npow = n_ab
        steps = CHUNK_A.bit_length() - 1
        for i in range(steps):
            yield
            u = u + _bdot(npow, u)
            if i + 1 < steps:
                npow = _bdot(npow, npow)
        yield
        w_rb = jnp.where(incl, _bdot_nt(r_st, b_st), 0.0)
        w_rk = jnp.where(incl, _bdot_nt(r_st, k_st), 0.0)
        yield
        y_st = _bdot_nt(r_st, S) + _bdot(w_rb, u) + _bdot(w_rk, v_st)
        y = y_st[0:CHUNK_A]
        for h in range(1, HEADS_PER_GROUP):
            y = y + y_st[h * CHUNK_A:(h + 1) * CHUNK_A]
        y_s[sl, ln] = y
        yield
        S_scr[g] = S * jnp.exp(cum_end) + _bdot_tn(u, bt_st) + _bdot_tn(v_st, kt_st)

    def chunk(j, carry):
        sl = pl.ds(pl.multiple_of(j * CHUNK_A, CHUNK_A), CHUNK_A)
        active = [chunk_group(sl, g) for g in range(gps)]
        while active:
            active = [gen for gen in active if next(gen, True) is None]
        return carry

    for j in range(nchunk):
        chunk(j, 0)

    y_ref[...] = _rwkv_post(y_s[...], bonus, zg[...], gng[...], gnb[...], seg).astype(y_ref.dtype)

    @pl.when(t == nt - 1)
    def _():
        s_ref[...] = S_scr[...]


def _rwkv_prompt(z2, zl, sh_rkv, sh_l, mu_rkv, mu_l, w0, a0, k_k, k_a, r_k, gng, gnb, ww_pad, wa_pad, tb, gps):
    b, t, _ = z2.shape
    da = w0.shape[1]
    wd = gps * GROUP
    ng = da // wd
    nt = t // tb
    per8 = tb // 8

    def zspec(off):
        return pl.BlockSpec((None, tb, wd), lambda bi, g, ti, off=off: (bi, ti, off + g))

    def pspec(off):
        return pl.BlockSpec((None, 8, wd),
                            lambda bi, g, ti, off=off: (bi, jnp.maximum(ti * per8 - 1, 0), off + g))

    def sspec(off):
        return pl.BlockSpec((None, 1, wd), lambda bi, g, ti, off=off: (bi, 0, off + g))

    def vspec(off=0):
        return pl.BlockSpec((1, wd), lambda bi, g, ti, off=off: (0, off + g))

    zl_spec = pl.BlockSpec((None, tb, LANES), lambda bi, g, ti: (bi, ti, 0))
    pl_spec = pl.BlockSpec((None, 8, LANES),
                           lambda bi, g, ti: (bi, jnp.maximum(ti * per8 - 1, 0), 0))
    sl_spec = pl.BlockSpec((None, 1, LANES), lambda bi, g, ti: (bi, 0, 0))
    ml_spec = pl.BlockSpec((1, LANES), lambda bi, g, ti: (0, 0))
    w_spec = pl.BlockSpec((LANES, wd), lambda bi, g, ti: (0, g))

    in_specs = [zspec(0), zspec(ng), zspec(2 * ng), zspec(3 * ng), zl_spec,
                pspec(0), pspec(ng), pspec(2 * ng), pl_spec,
                sspec(0), sspec(ng), sspec(2 * ng), sl_spec,
                vspec(0), vspec(ng), vspec(2 * ng), ml_spec,
                vspec(), vspec(), vspec(), vspec(), vspec(), vspec(), vspec(), w_spec, w_spec]
    blk = lambda: pltpu.VMEM((tb, wd), F32)
    return pl.pallas_call(
        _rwkv_prompt_kernel,
        grid=(b, ng, nt),
        in_specs=in_specs,
        out_specs=[pl.BlockSpec((None, tb, wd), lambda bi, g, ti: (bi, ti, g)),
                   pl.BlockSpec((None, gps, GROUP, GROUP), lambda bi, g, ti: (bi, g, 0, 0))],
        out_shape=[jax.ShapeDtypeStruct((b, t, da), BF16),
                   jax.ShapeDtypeStruct((b, ng * gps, GROUP, GROUP), F32)],
        scratch_shapes=[pltpu.VMEM((gps, GROUP, GROUP), F32)] + [blk() for _ in range(8)],
        compiler_params=_cparams(("parallel", "parallel", "arbitrary")),
        name="rwkv_prompt",
    )(z2, z2, z2, z2, zl, z2, z2, z2, zl, sh_rkv, sh_rkv, sh_rkv, sh_l,
      mu_rkv, mu_rkv, mu_rkv, mu_l, w0, a0, k_k, k_a, r_k, gng, gnb, ww_pad, wa_pad)


def _rwkv_dec_pre_kernel(zr, zk, zv, zg, zl, pr, pk, pv, plr, mur, muk, muv, mul,
                         w0, a0, k_k, k_a, r_k, ww, wa,
                         r_o, w_o, k_o, v_o, al_o, be_o, bo_o, sg_o):
    seg = _seg_ones(zr.shape[1], HD_A)
    sh = lambda z, p, mu: z[...] + mu[...] * (p[...] - z[...])
    xr, xk, xv, xl = sh(zr, pr, mur), sh(zk, pk, muk), sh(zv, pv, muv), sh(zl, plr, mul)
    r, k2, v, logw, kk, a, bonus = _rwkv_vectors(
        xr, xk, xv, xl, w0[...], a0[...], k_k[...], k_a[...], r_k[...], ww[...], wa[...], seg)
    r_o[...] = r.T
    w_o[...] = jnp.exp(logw).T
    k_o[...] = k2.T
    v_o[...] = v.T
    al_o[...] = (-kk).T
    be_o[...] = (kk * a).T
    bo_o[...] = bonus
    sg_o[...] = _silu(zg[...])


def _rwkv_dec_pre(z2, zl, prev_rkv, prev_l, mu_rkv, mu_l, w0, a0, k_k, k_a, r_k, ww_pad, wa_pad):
    b, _ = z2.shape
    da = w0.shape[1]
    c = lambda w, j: pl.BlockSpec((b, w), lambda i, j=j: (0, j))
    in_specs = [c(da, 0), c(da, 1), c(da, 2), c(da, 3), c(LANES, 0),
                c(da, 0), c(da, 1), c(da, 2), c(LANES, 0),
                pl.BlockSpec((1, da), lambda i: (0, 0)), pl.BlockSpec((1, da), lambda i: (0, 1)),
                pl.BlockSpec((1, da), lambda i: (0, 2)), pl.BlockSpec((1, LANES), lambda i: (0, 0))]
    in_specs += [pl.BlockSpec((1, da), lambda i: (0, 0))] * 5
    in_specs += [pl.BlockSpec((LANES, da), lambda i: (0, 0))] * 2
    return pl.pallas_call(
        _rwkv_dec_pre_kernel,
        grid=(1,),
        in_specs=in_specs,
        out_specs=[pl.BlockSpec((da, b), lambda i: (0, 0))] * 6 + [pl.BlockSpec((b, da), lambda i: (0, 0))] * 2,
        out_shape=[jax.ShapeDtypeStruct((da, b), F32)] * 6 + [jax.ShapeDtypeStruct((b, da), F32)] * 2,
        compiler_params=_cparams(("arbitrary",)),
        name="rwkv_dec_pre",
    )(z2, z2, z2, z2, zl, prev_rkv, prev_rkv, prev_rkv, prev_l, mu_rkv, mu_rkv, mu_rkv, mu_l,
      w0, a0, k_k, k_a, r_k, ww_pad, wa_pad)


V_ROWS = 8


def _rwkv_dec_state_kernel(s_ref, w_ref, al_ref, be_ref, k_ref, r_ref, v_ref, so_ref, y_ref):
    hb, n, _, _ = s_ref.shape
    for h in range(hb):
        w, al, be, k2, r = w_ref[h], al_ref[h], be_ref[h], k_ref[h], r_ref[h]

        def v_rows(i, carry, h=h, w=w, al=al, be=be, k2=k2, r=r):
            sl = pl.ds(pl.multiple_of(i * V_ROWS, V_ROWS), V_ROWS)
            S = s_ref[h, sl]
            sa = jnp.sum(S * al[None], axis=1)
            vv = v_ref[h, sl, :]
            sn = S * w[None] + sa[:, None, :] * be[None] + vv[:, None, :] * k2[None]
            so_ref[h, sl] = sn
            y_ref[h, sl, :] = jnp.sum(sn * r[None], axis=1)
            return carry

        lax.fori_loop(0, n // V_ROWS, v_rows, 0)


def _rwkv_dec_state(s_hvkb, w, al, be, k2, r, v, hb):
    h, n, _, b = s_hvkb.shape
    sspec = pl.BlockSpec((hb, n, n, b), lambda i: (i, 0, 0, 0))
    rspec = pl.BlockSpec((hb, n, b), lambda i: (i, 0, 0))
    return pl.pallas_call(
        _rwkv_dec_state_kernel,
        grid=(h // hb,),
        in_specs=[sspec] + [rspec] * 6,
        out_specs=[sspec, rspec],
        out_shape=[jax.ShapeDtypeStruct(s_hvkb.shape, F32), jax.ShapeDtypeStruct((h, n, b), F32)],
        compiler_params=_cparams(("parallel",)),
        name="rwkv_dec_state",
    )(s_hvkb, w, al, be, k2, r, v)


def _rotate_pairs(x, cos, sin_signed):
    n = x.shape[-1]
    lane = lax.broadcasted_iota(jnp.int32, x.shape, x.ndim - 1)
    nxt = pltpu.roll(x, n - 1, x.ndim - 1)
    prv = pltpu.roll(x, 1, x.ndim - 1)
    swapped = jnp.where(lane % 2 == 0, nxt, prv)
    return x * cos + swapped * sin_signed


def _head_norm_b(y, g, b):
    mu = jnp.mean(y, axis=-1, keepdims=True)
    yc = y - mu
    var = jnp.mean(yc * yc, axis=-1, keepdims=True)
    return yc * lax.rsqrt(var + GN_EPS_B) * g + b


def _ret_prompt_kernel(lg_ref, q_ref, k_ref, v_ref, g_ref, cos_ref, sin_ref, gng, gnb,
                       y_ref, s_ref, S_scr):
    c = pl.program_id(1)
    nc = pl.num_programs(1)
    L = q_ref.shape[0]
    d = cos_ref.shape[1]
    nh = q_ref.shape[1] // d

    @pl.when(c == 0)
    def _():
        S_scr[...] = jnp.zeros_like(S_scr)

    cos, sin = cos_ref[...], sin_ref[...]
    ii = lax.broadcasted_iota(jnp.int32, (L, L), 0)
    jj = lax.broadcasted_iota(jnp.int32, (L, L), 1)
    diff = (ii - jj).astype(F32)
    idx = lax.broadcasted_iota(jnp.int32, (L, 1), 0).astype(F32)

    def head(h):
        ln = slice(h * d, (h + 1) * d)
        lg = lg_ref[h, 0:1, 0:1]
        q = _rotate_pairs(q_ref[:, ln], cos, sin)
        k = _rotate_pairs(k_ref[:, ln], cos, sin) * (d ** -0.5)
        v = v_ref[:, ln]
        dmask = jnp.where(diff >= 0, jnp.exp(jnp.maximum(diff, 0.0) * lg), 0.0)
        S = S_scr[h]
        yield
        scores = _bdot_nt(q, k) * dmask
        cross = _bdot(q, S) * jnp.exp((idx + 1.0) * lg)
        yield
        inner = _bdot(scores, v)
        kdec = jnp.exp((L - 1.0 - idx) * lg)
        S_scr[h] = S * jnp.exp(L * lg) + _bdot_tn(k * kdec, v)
        yield
        y = _head_norm_b(inner + cross, gng[:, ln], gnb[:, ln]) * _silu(g_ref[:, ln])
        y_ref[:, ln] = y.astype(y_ref.dtype)

    active = [head(h) for h in range(nh)]
    while active:
        active = [gen for gen in active if next(gen, True) is None]

    @pl.when(c == nc - 1)
    def _():
        s_ref[...] = S_scr[...]


def _ret_prompt(z2, lg_tile, cos, sin, gng, gnb, col0):
    b, t, _ = z2.shape
    d = cos.shape[1]
    db = gng.shape[1]
    nh = db // d
    L = _pick_tile(t, (CHUNK_B,))
    nchunk = t // L
    c0 = col0 // db
    zspec = lambda off: pl.BlockSpec((None, L, db), lambda bi, ci, off=off: (bi, ci, c0 + off))
    tspec = pl.BlockSpec((L, d), lambda bi, ci: (ci, 0))
    vspec = pl.BlockSpec((1, db), lambda bi, ci: (0, 0))
    return pl.pallas_call(
        _ret_prompt_kernel,
        grid=(b, nchunk),
        in_specs=[pl.BlockSpec(lg_tile.shape, lambda bi, ci: (0, 0, 0)),
                  zspec(0), zspec(1), zspec(2), zspec(3), tspec, tspec, vspec, vspec],
        out_specs=[pl.BlockSpec((None, L, db), lambda bi, ci: (bi, ci, 0)),
                   pl.BlockSpec((None, nh, d, d), lambda bi, ci: (bi, 0, 0, 0))],
        out_shape=[jax.ShapeDtypeStruct((b, t, db), BF16),
                   jax.ShapeDtypeStruct((b, nh, d, d), F32)],
        scratch_shapes=[pltpu.VMEM((nh, d, d), F32)],
        compiler_params=_cparams(("parallel", "arbitrary")),
        name="ret_prompt",
    )(lg_tile, z2, z2, z2, z2, cos, sin, gng, gnb)


def _ret_dec_pre_kernel(zq, zk, zv, zg, cos_ref, sin_ref, q_o, k_o, in_o, sg_o):
    d = cos_ref.shape[1] // H_B
    seg = _seg_ones(zq.shape[1], d)
    cos, sin = cos_ref[...], sin_ref[...]
    q = _rotate_pairs(zq[...], cos, sin)
    k = _rotate_pairs(zk[...], cos, sin) * (d ** -0.5)
    q_o[...] = q
    k_o[...] = k
    in_o[...] = _dot_exact_rhs(q * k, seg) * zv[...]
    sg_o[...] = _silu(zg[...])


def _ret_dec_pre(z2, cos, sin, col0):
    b, _ = z2.shape
    db = cos.shape[1]
    c0 = col0 // db
    c = lambda j: pl.BlockSpec((b, db), lambda i, j=j: (0, c0 + j))
    t = pl.BlockSpec((1, db), lambda i: (0, 0))
    return pl.pallas_call(
        _ret_dec_pre_kernel,
        grid=(1,),
        in_specs=[c(0), c(1), c(2), c(3), t, t],
        out_specs=[pl.BlockSpec((b, db), lambda i: (0, 0))] * 4,
        out_shape=[jax.ShapeDtypeStruct((b, db), F32)] * 4,
        compiler_params=_cparams(("arbitrary",)),
        name="ret_dec_pre",
    )(z2, z2, z2, z2, cos, sin)


def _ret_dec_state_kernel(lg_ref, s_ref, q_ref, k_ref, v_ref, so_ref, cr_ref):
    bb, nh, d, _ = s_ref.shape
    rowid = lax.broadcasted_iota(jnp.int32, (bb, 1), 0)
    for h in range(nh):
        ln = slice(h * d, (h + 1) * d)
        gamma = jnp.exp(lg_ref[h, 0:1, 0:1])
        q, k, v = q_ref[:, ln], k_ref[:, ln], v_ref[:, ln]
        cross = jnp.zeros((bb, d), F32)
        for b in range(bb):
            sel = rowid == b
            S = s_ref[b, h]
            cross = cross + _bdot(jnp.where(sel, q, 0.0), S)
            so_ref[b, h] = S * gamma + _bdot_tn(jnp.where(sel, k, 0.0), v)
        cr_ref[:, ln] = cross * gamma


def _ret_dec_state(lg_tile, s0, q, k, v, bb):
    b, nh, d, _ = s0.shape
    sspec = pl.BlockSpec((bb, nh, d, d), lambda i: (i, 0, 0, 0))
    rspec = pl.BlockSpec((bb, nh * d), lambda i: (i, 0))
    return pl.pallas_call(
        _ret_dec_state_kernel,
        grid=(b // bb,),
        in_specs=[pl.BlockSpec(lg_tile.shape, lambda i: (0, 0, 0)), sspec, rspec, rspec, rspec],
        out_specs=[sspec, rspec],
        out_shape=[jax.ShapeDtypeStruct(s0.shape, F32), jax.ShapeDtypeStruct((b, nh * d), F32)],
        compiler_params=_cparams(("parallel",)),
        name="ret_dec_state",
    )(lg_tile, s0, q, k, v)


def _dec_post_kernel(ya_ref, bo_ref, sga_ref, gag, gab, in_ref, cr_ref, sgb_ref, gbg, gbb,
                     oa_ref, ob_ref):
    da = ya_ref.shape[0]
    db = in_ref.shape[1]
    d = db // H_B
    sega = _seg_ones(da, HD_A)
    segb = _seg_ones(db, d)
    y = ya_ref[...].T
    mu = _dot_exact_rhs(y, sega) * (1.0 / HD_A)
    yc = y - mu
    var = _dot_exact_rhs(yc * yc, sega) * (1.0 / HD_A)
    hn = yc * lax.rsqrt(var + GN_EPS_A) * gag[...] + gab[...]
    oa_ref[...] = ((hn + bo_ref[...]) * sga_ref[...]).astype(oa_ref.dtype)
    y = in_ref[...] + cr_ref[...]
    mu = _dot_exact_rhs(y, segb) * (1.0 / d)
    yc = y - mu
    var = _dot_exact_rhs(yc * yc, segb) * (1.0 / d)
    hn = yc * lax.rsqrt(var + GN_EPS_B) * gbg[...] + gbb[...]
    ob_ref[...] = (hn * sgb_ref[...]).astype(ob_ref.dtype)


def _dec_post(ya_t, bonus, sga, gag, gab, inner, cross, sgb, gbg, gbb):
    da, b = ya_t.shape
    db = inner.shape[1]
    ft = pl.BlockSpec((da, b), lambda i: (0, 0))
    fa = pl.BlockSpec((b, da), lambda i: (0, 0))
    fb = pl.BlockSpec((b, db), lambda i: (0, 0))
    va = pl.BlockSpec((1, da), lambda i: (0, 0))
    vb = pl.BlockSpec((1, db), lambda i: (0, 0))
    return pl.pallas_call(
        _dec_post_kernel,
        grid=(1,),
        in_specs=[ft, fa, fa, va, va, fb, fb, fb, vb, vb],
        out_specs=[fa, fb],
        out_shape=[jax.ShapeDtypeStruct((b, da), BF16), jax.ShapeDtypeStruct((b, db), BF16)],
        compiler_params=_cparams(("arbitrary",)),
        name="dec_post",
    )(ya_t, bonus, sga, gag, gab, inner, cross, sgb, gbg, gbb)


def _rot_tables(pos, d):
    angle = 1.0 / (ROPE_BASE ** jnp.linspace(0.0, 1.0, d // 2, dtype=F32))
    theta = pos[:, None] * angle[None, :]
    cos = jnp.repeat(jnp.cos(theta), 2, axis=-1)
    sin = jnp.repeat(jnp.sin(theta), 2, axis=-1)
    sign = jnp.tile(jnp.array([-1.0, 1.0], F32), d // 2)
    return cos, sin * sign


def _diag_blocks(s_bd):
    b, g = s_bd.shape[:2]
    s6 = s_bd.reshape(b, g, HEADS_PER_GROUP, HD_A, HEADS_PER_GROUP, HD_A)
    blocks = jnp.stack([s6[:, :, i, :, i, :] for i in range(HEADS_PER_GROUP)], axis=2)
    return blocks.reshape(b, g * HEADS_PER_GROUP, HD_A, HD_A)


def _pick_tile(n, pref):
    for t in pref:
        if n % t == 0:
            return t
    return n


def kernel(x_prompt, x_sample, p_prompt, p_sample, state_wkv, state_shift, state_ret, g_ln, w_in,
           mu_shift, w0, w_wB, a0, w_aB, k_k, k_a, r_k, gn_a_g, gn_a_b, gn_b_g, gn_b_b, w_out, w_ple,
           w_ple_gate, g_final):
    depth = g_ln.shape[0]
    assert depth == 1
    bp, tp, d_model = x_prompt.shape
    bs, ts, _ = x_sample.shape
    assert ts == 1
    da = w0.shape[1]
    db = gn_b_g.shape[1]
    hd_b = db // H_B
    n_rkv = 3 * da
    a_shift = n_rkv + 2 * LORA
    assert LORA * 2 == LANES and a_shift == mu_shift.shape[1]
    assert tp % CHUNK_A == 0 and da % GROUP == 0

    w = w_in[0]
    w_main = jnp.concatenate([w[:, :n_rkv], w[:, a_shift:]], axis=1).astype(BF16)
    w_lora = w[:, n_rkv:a_shift].astype(BF16)
    col_b = n_rkv + da
    mu = mu_shift[0]
    mu_rkv = mu[:n_rkv].reshape(1, n_rkv)
    mu_l = mu[n_rkv:].reshape(1, LANES)
    zpad = jnp.zeros((LORA, da), F32)
    ww_pad = jnp.concatenate([w_wB[0], zpad], axis=0).astype(BF16)
    wa_pad = jnp.concatenate([zpad, w_aB[0]], axis=0).astype(BF16)
    v1 = lambda a: a[0].reshape(1, -1)
    w0_, a0_, kk_, ka_, rk_ = v1(w0), v1(a0), v1(k_k), v1(k_a), v1(r_k)
    gag, gab, gbg, gbb = v1(gn_a_g), v1(gn_a_b), v1(gn_b_g), v1(gn_b_b)
    woa = w_out[0][:da].astype(BF16)
    wob = w_out[0][da:].astype(BF16)
    wg = w_ple_gate[0].astype(BF16)
    wp = w_ple[0].astype(BF16)
    lg = jnp.log(1.0 - jnp.exp2(-5.0 - jnp.arange(H_B, dtype=F32)))
    lg_tile = jnp.broadcast_to(lg[:, None, None], (H_B, 8, LANES))

    tn = _pick_tile(w_main.shape[1], (1024, 512, 256, 128))

    xp = x_prompt.reshape(bp * tp, d_model)
    zp, zlp = _proj(xp, g_ln[0], w_main, w_lora, _pick_tile(bp * tp, (1024, 512, 256, 128)), tn)
    zp3 = zp.reshape(bp, tp, -1)
    zlp3 = zlp.reshape(bp, tp, LANES)
    sh0 = jnp.zeros((bp, 1, a_shift), F32)
    tb = _pick_tile(tp, (256, 128, 64))
    ya_p, s_bd = _rwkv_prompt(zp3, zlp3, sh0[:, :, :n_rkv], sh0[:, :, n_rkv:], mu_rkv, mu_l, w0_, a0_, kk_,
                              ka_, rk_, gag, gab, ww_pad, wa_pad, tb, da // GROUP)
    cos_p, sin_p = _rot_tables(jnp.arange(tp, dtype=F32), hd_b)
    yb_p, ret_p = _ret_prompt(zp3, lg_tile, cos_p, sin_p, gbg, gbb, col_b)
    y_prompt = _out_block(xp, ya_p.reshape(bp * tp, da), yb_p.reshape(bp * tp, db),
                          p_prompt[0].reshape(bp * tp, -1), woa, wob, wg, wp, g_final,
                          _pick_tile(bp * tp, (256, 128))).reshape(bp, tp, d_model)
    wkv_p = _diag_blocks(s_bd)
    shift_p = jnp.concatenate([zp3[:, -1, :n_rkv], zlp3[:, -1]], axis=-1)

    xs = x_sample.reshape(bs, d_model)
    zs, zls = _proj(xs, g_ln[0], w_main, w_lora, bs, tn)
    prev = state_shift[0]
    r, wdec, k2, v, al, be, bonus, sga = _rwkv_dec_pre(
        zs, zls, prev[:, :n_rkv], prev[:, n_rkv:], mu_rkv, mu_l, w0_, a0_, kk_, ka_, rk_, ww_pad, wa_pad)
    h_a = da // HD_A
    hs = lambda a: a.reshape(h_a, HD_A, bs)
    s_hvkb = jnp.transpose(state_wkv[0], (1, 2, 3, 0))
    s_new, y_t = _rwkv_dec_state(s_hvkb, hs(wdec), hs(al), hs(be), hs(k2), hs(r), hs(v),
                                 _pick_tile(h_a, (2, 1)))
    wkv_s = jnp.transpose(s_new, (3, 0, 1, 2))
    cos_s, sin_s = _rot_tables((PAST_LEN + jnp.arange(ts)).astype(F32), hd_b)
    cos_s = jnp.tile(cos_s, (1, H_B))
    sin_s = jnp.tile(sin_s, (1, H_B))
    q_s, k_s, inner, sgb = _ret_dec_pre(zs, cos_s, sin_s, col_b)
    ret_s, cross = _ret_dec_state(lg_tile, state_ret[0], q_s, k_s,
                                  zs[:, col_b + 2 * db:col_b + 3 * db], _pick_tile(bs, (8,)))
    ya_s, yb_s = _dec_post(y_t.reshape(da, bs), bonus, sga, gag, gab, inner, cross, sgb, gbg, gbb)
    y_sample = _out_block(xs, ya_s, yb_s, p_sample[0].reshape(bs, -1), woa, wob, wg, wp, g_final,
                          bs).reshape(bs, 1, d_model)
    shift_s = jnp.concatenate([zs[:, :n_rkv], zls], axis=-1)

    return (y_prompt, y_sample, wkv_p[None], shift_p[None], ret_p[None],
            wkv_s[None], shift_s[None], ret_s[None])
```

```python
import functools

import jax
import jax.numpy as jnp
from jax import lax
from jax.experimental import pallas as pl
from jax.experimental.pallas import tpu as pltpu

F32 = jnp.float32
BF16 = jnp.bfloat16

HD_A = 64
LORA = 64
H_B = 4
ROPE_BASE = 10000.0
RMS_EPS = 1e-6
GN_EPS_A = HD_A * 1e-5
GN_EPS_B = 1e-5
PAST_LEN = 16384
DECAY_SCALE = 0.6065306597126334

LANES = 128
GROUP = 256
HEADS_PER_GROUP = GROUP // HD_A
CHUNK_A = 64
STACK = HEADS_PER_GROUP * CHUNK_A
CHUNK_B = 256
VMEM_LIMIT = 48 * 1024 * 1024


def _cparams(sem):
    return pltpu.CompilerParams(dimension_semantics=sem, vmem_limit_bytes=VMEM_LIMIT)


def _bdot(a, b):
    return jnp.dot(a.astype(BF16), b.astype(BF16), preferred_element_type=F32)


def _bdot_nt(a, b):
    return lax.dot_general(a.astype(BF16), b.astype(BF16), (((1,), (1,)), ((), ())),
                           preferred_element_type=F32)


def _bdot_tn(a, b):
    return lax.dot_general(a.astype(BF16), b.astype(BF16), (((0,), (0,)), ((), ())),
                           preferred_element_type=F32)


def _split3(x):
    hi = x.astype(BF16)
    r1 = x - hi.astype(F32)
    mid = r1.astype(BF16)
    lo = (r1 - mid.astype(F32)).astype(BF16)
    return hi, mid, lo


def _dot_exact_rhs(x, m01):
    hi = x.astype(BF16)
    lo = (x - hi.astype(F32)).astype(BF16)
    d = lambda t: jnp.dot(t, m01, preferred_element_type=F32)
    return d(hi) + d(lo)


def _dot_exact_lhs(m01, x):
    hi, mid, lo = _split3(x)
    d = lambda t: jnp.dot(m01, t, preferred_element_type=F32)
    return d(hi) + d(mid) + d(lo)


def _segsum(x, seg01):
    n = seg01.shape[0]
    parts = [_dot_exact_rhs(x[:, i:i + n], seg01) for i in range(0, x.shape[1], n)]
    return parts[0] if len(parts) == 1 else jnp.concatenate(parts, axis=1)


def _seg_ones(n, seg):
    i = lax.broadcasted_iota(jnp.int32, (n, n), 0) // seg
    j = lax.broadcasted_iota(jnp.int32, (n, n), 1) // seg
    return jnp.where(i == j, 1.0, 0.0).astype(BF16)


def _sigmoid(x):
    return 1.0 / (1.0 + jnp.exp(-x))


def _silu(x):
    return x * _sigmoid(x)


def _proj_kernel(x_ref, g_ref, w_ref, wl_ref, o_ref, ol_ref, u_ref):
    @pl.when(pl.program_id(1) == 0)
    def _():
        x = x_ref[...]
        ms = jnp.mean(x * x, axis=-1, keepdims=True)
        u = (x * lax.rsqrt(ms + RMS_EPS) * g_ref[...]).astype(BF16)
        u_ref[...] = u
        ol_ref[...] = jnp.dot(u, wl_ref[...], preferred_element_type=F32)

    o_ref[...] = jnp.dot(u_ref[...], w_ref[...], preferred_element_type=F32)


def _wprep_kernel(w_ref, o_ref):
    o_ref[...] = w_ref[...].astype(BF16)


def _wprep(w, lo, hi):
    d, n = w.shape
    nblk, lo_b, wid = n // LANES, lo // LANES, (hi - lo) // LANES
    src = lambda j: (0, jnp.where(j < lo_b, j, jnp.where(j < nblk - wid, j + wid, j - (nblk - wid) + lo_b)))
    return pl.pallas_call(
        _wprep_kernel,
        grid=(nblk,),
        in_specs=[pl.BlockSpec((d, LANES), src)],
        out_specs=pl.BlockSpec((d, LANES), lambda j: (0, j)),
        out_shape=jax.ShapeDtypeStruct((d, n), BF16),
        compiler_params=_cparams(("parallel",)),
        name="w_prep",
    )(w)


def _proj(x2d, g_ln, w_all, nl, tm, tn):
    m, d = x2d.shape
    nc = w_all.shape[1] - nl
    return pl.pallas_call(
        _proj_kernel,
        grid=(m // tm, nc // tn),
        in_specs=[pl.BlockSpec((tm, d), lambda i, j: (i, 0)),
                  pl.BlockSpec((1, d), lambda i, j: (0, 0)),
                  pl.BlockSpec((d, tn), lambda i, j: (0, j)),
                  pl.BlockSpec((d, nl), lambda i, j: (0, nc // nl))],
        out_specs=[pl.BlockSpec((tm, tn), lambda i, j: (i, j)),
                   pl.BlockSpec((tm, nl), lambda i, j: (i, 0))],
        out_shape=[jax.ShapeDtypeStruct((m, nc), F32), jax.ShapeDtypeStruct((m, nl), F32)],
        scratch_shapes=[pltpu.VMEM((tm, d), BF16)],
        compiler_params=_cparams(("parallel", "arbitrary")),
        name="in_proj",
    )(x2d, g_ln.reshape(1, d), w_all, w_all)


def _out_kernel(x_ref, ya_ref, yb_ref, p_ref, woa_ref, wob_ref, wg_ref, wp_ref, gf_ref, o_ref):
    h2 = (x_ref[...]
          + jnp.dot(ya_ref[...], woa_ref[...], preferred_element_type=F32)
          + jnp.dot(yb_ref[...], wob_ref[...], preferred_element_type=F32))
    gate = _sigmoid(jnp.dot(h2.astype(BF16), wg_ref[...], preferred_element_type=F32))
    ple = jnp.dot(p_ref[...].astype(BF16), wp_ref[...], preferred_element_type=F32)
    h3 = h2 + gate * ple
    ms = jnp.mean(h3 * h3, axis=-1, keepdims=True)
    o_ref[...] = h3 * lax.rsqrt(ms + RMS_EPS) * gf_ref[...]


def _out_block(x2d, ya, yb, p2d, woa, wob, wg, wp, g_final, tm):
    m, d = x2d.shape
    da, db, dp = ya.shape[1], yb.shape[1], p2d.shape[1]
    row = lambda w: pl.BlockSpec((tm, w), lambda i: (i, 0))
    whole = lambda a: pl.BlockSpec(a.shape, lambda i: (0, 0), pipeline_mode=pl.Buffered(1))
    gf = g_final.reshape(1, d)
    return pl.pallas_call(
        _out_kernel,
        grid=(m // tm,),
        in_specs=[row(d), row(da), row(db), row(dp), whole(woa), whole(wob), whole(wg), whole(wp),
                  whole(gf)],
        out_specs=row(d),
        out_shape=jax.ShapeDtypeStruct((m, d), F32),
        compiler_params=_cparams(("parallel",)),
        name="out_block",
    )(x2d, ya, yb, p2d, woa, wob, wg, wp, gf)


def _rwkv_vectors(xr, xk, xv, xl, w0, a0, k_k, k_a, r_k, ww_pad, wa_pad, seg):
    lw = _bdot(jnp.tanh(xl), ww_pad)
    la = _bdot(xl, wa_pad)
    logw = -DECAY_SCALE * _sigmoid(w0 + lw)
    a = _sigmoid(a0 + la)
    kk = xk * k_k
    kk = kk * jnp.minimum(lax.rsqrt(_segsum(kk * kk, seg)), 1e12)
    k2 = xk * (1.0 + (a - 1.0) * k_a)
    bonus = _segsum(xr * k2 * r_k, seg) * xv
    return xr, k2, xv, logw, kk, a, bonus


def _rwkv_post(y, bonus, ga, gn_g, gn_b, seg):
    inv = 1.0 / HD_A
    mu = _segsum(y, seg) * inv
    yc = y - mu
    var = _segsum(yc * yc, seg) * inv
    hn = yc * lax.rsqrt(var + GN_EPS_A) * gn_g + gn_b
    return (hn + bonus) * _silu(ga)


def _rwkv_prompt_kernel(zr, zk, zv, zg, zl, pr, pk, pv, plr, sr, sk, sv, slr,
                        mur, muk, muv, mul, w0, a0, k_k, k_a, r_k, gng, gnb, ww, wa,
                        y_ref, s_ref,
                        S_scr, r_s, k_s, v_s, lw_s, kk_s, a_s, y_s, cum_s):
    t = pl.program_id(2)
    nt = pl.num_programs(2)
    tb = zr.shape[0]
    gps = zr.shape[1] // GROUP
    nchunk = tb // CHUNK_A
    seg = _seg_ones(GROUP, HD_A)

    @pl.when(t == 0)
    def _():
        S_scr[...] = jnp.zeros_like(S_scr)

    row = lax.broadcasted_iota(jnp.int32, (tb, 1), 0)

    def shifted(z_ref, p_ref, s0_ref, mu_ref):
        x = z_ref[...]
        last = jnp.where(t == 0, s0_ref[...], p_ref[7:8, :])
        prev = jnp.where(row == 0, last, pltpu.roll(x, 1, 0))
        return x + mu_ref[...] * (prev - x)

    xr = shifted(zr, pr, sr, mur)
    xk = shifted(zk, pk, sk, muk)
    xv = shifted(zv, pv, sv, muv)
    xl = shifted(zl, plr, slr, mul)
    r, k2, v, logw, kk, a, bonus = _rwkv_vectors(
        xr, xk, xv, xl, w0[...], a0[...], k_k[...], k_a[...], r_k[...], ww[...], wa[...], seg)
    r_s[...] = r
    k_s[...] = k2
    v_s[...] = v
    lw_s[...] = logw
    kk_s[...] = kk
    a_s[...] = a
    bi = lax.broadcasted_iota(jnp.int32, (tb, tb), 0)
    bj = lax.broadcasted_iota(jnp.int32, (tb, tb), 1)
    tri_blk = jnp.where(((bi // CHUNK_A) == (bj // CHUNK_A)) & (bi >= bj), 1.0, 0.0).astype(BF16)
    cum_s[...] = _dot_exact_lhs(tri_blk, logw)

    lane = lax.broadcasted_iota(jnp.int32, (1, GROUP), 1) // HD_A
    masks = [jnp.where(lane == h, 1.0, 0.0) for h in range(HEADS_PER_GROUP)]
    si =lax.broadcasted_iota(jnp.int32, (STACK, STACK), 0)
    sj = lax.broadcasted_iota(jnp.int32, (STACK, STACK), 1)
    same = (si // CHUNK_A) == (sj // CHUNK_A)
    strict = same & ((si % CHUNK_A) > (sj % CHUNK_A))
    incl = same & ((si % CHUNK_A) >= (sj % CHUNK_A))

    def stack(x):
        return jnp.concatenate([x * m for m in masks], axis=0)

    def chunk_group(sl, g):
        ln = slice(g * GROUP, (g + 1) * GROUP)
        rc, kc, vc, lwc, kkc, ac = r_s[sl, ln], k_s[sl, ln], v_s[sl, ln], lw_s[sl, ln], kk_s[sl, ln], a_s[sl, ln]
        cum = cum_s[sl, ln]
        e_pos = jnp.exp(cum)
        e_neg = jnp.exp(-cum)
        cum_end = cum[CHUNK_A - 1:CHUNK_A, :]
        e_tail = jnp.exp(cum_end - cum)
        beta = kkc * ac
        a_st = stack(-kkc * jnp.exp(cum - lwc))
        r_st = stack(rc * e_pos)
        b_st = stack(beta * e_neg)
        k_st = stack(kc * e_neg)
        v_st = stack(vc)
        bt_st = stack(beta * e_tail)
        kt_st = stack(kc * e_tail)

        S = S_scr[g]
        yield
        n_ab = jnp.where(strict, _bdot_nt(a_st, b_st), 0.0)
        n_ak = jnp.where(strict, _bdot_nt(a_st, k_st), 0.0)
        yield
        u = _bdot_nt(a_st, S) + _bdot(n_ak, v_st)
        npow = n_ab
        steps = CHUNK_A.bit_length() - 1
        for i in range(steps):
            yield
            u = u + _bdot(npow, u)
            if i + 1 < steps:
                npow = _bdot(npow, npow)
        yield
        w_rb = jnp.where(incl, _bdot_nt(r_st, b_st), 0.0)
        w_rk = jnp.where(incl, _bdot_nt(r_st, k_st), 0.0)
        yield
        y_st = _bdot_nt(r_st, S) + _bdot(w_rb, u) + _bdot(w_rk, v_st)
        y = y_st[0:CHUNK_A]
        for h in range(1, HEADS_PER_GROUP):
            y = y + y_st[h * CHUNK_A:(h + 1) * CHUNK_A]
        y_s[sl, ln] = y
        yield
        S_scr[g] = S * jnp.exp(cum_end) + _bdot_tn(u, bt_st) + _bdot_tn(v_st, kt_st)

    def chunk(j, carry):
        sl = pl.ds(pl.multiple_of(j * CHUNK_A, CHUNK_A), CHUNK_A)
        active = [chunk_group(sl, g) for g in range(gps)]
        while active:
            active = [gen for gen in active if next(gen, True) is None]
        return carry

    for j in range(nchunk):
        chunk(j, 0)

    y_ref[...] = _rwkv_post(y_s[...], bonus, zg[...], gng[...], gnb[...], seg).astype(y_ref.dtype)

    @pl.when(t == nt - 1)
    def _():
        s_ref[...] = S_scr[...]


def _rwkv_prompt(z2, zl, sh_rkv, sh_l, mu_rkv, mu_l, w0, a0, k_k, k_a, r_k, gng, gnb, ww_pad, wa_pad, tb, gps):
    b, t, _ = z2.shape
    da = w0.shape[1]
    wd = gps * GROUP
    ng = da // wd
    nt = t // tb
    per8 = tb // 8

    def zspec(off):
        return pl.BlockSpec((None, tb, wd), lambda bi, g, ti, off=off: (bi, ti, off + g))

    def pspec(off):
        return pl.BlockSpec((None, 8, wd),
                            lambda bi, g, ti, off=off: (bi, jnp.maximum(ti * per8 - 1, 0), off + g))

    def sspec(off):
        return pl.BlockSpec((None, 1, wd), lambda bi, g, ti, off=off: (bi, 0, off + g))

    def vspec(off=0):
        return pl.BlockSpec((1, wd), lambda bi, g, ti, off=off: (0, off + g))

    zl_spec = pl.BlockSpec((None, tb, LANES), lambda bi, g, ti: (bi, ti, 0))
    pl_spec = pl.BlockSpec((None, 8, LANES),
                           lambda bi, g, ti: (bi, jnp.maximum(ti * per8 - 1, 0), 0))
    sl_spec = pl.BlockSpec((None, 1, LANES), lambda bi, g, ti: (bi, 0, 0))
    ml_spec = pl.BlockSpec((1, LANES), lambda bi, g, ti: (0, 0))
    w_spec = pl.BlockSpec((LANES, wd), lambda bi, g, ti: (0, g))

    in_specs = [zspec(0), zspec(ng), zspec(2 * ng), zspec(3 * ng), zl_spec,
                pspec(0), pspec(ng), pspec(2 * ng), pl_spec,
                sspec(0), sspec(ng), sspec(2 * ng), sl_spec,
                vspec(0), vspec(ng), vspec(2 * ng), ml_spec,
                vspec(), vspec(), vspec(), vspec(), vspec(), vspec(), vspec(), w_spec, w_spec]
    blk = lambda: pltpu.VMEM((tb, wd), F32)
    return pl.pallas_call(
        _rwkv_prompt_kernel,
        grid=(b, ng, nt),
        in_specs=in_specs,
        out_specs=[pl.BlockSpec((None, tb, wd), lambda bi, g, ti: (bi, ti, g)),
                   pl.BlockSpec((None, gps, GROUP, GROUP), lambda bi, g, ti: (bi, g, 0, 0))],
        out_shape=[jax.ShapeDtypeStruct((b, t, da), BF16),
                   jax.ShapeDtypeStruct((b, ng * gps, GROUP, GROUP), F32)],
        scratch_shapes=[pltpu.VMEM((gps, GROUP, GROUP), F32)] + [blk() for _ in range(8)],
        compiler_params=_cparams(("parallel", "parallel", "arbitrary")),
        name="rwkv_prompt",
    )(z2, z2, z2, z2, zl, z2, z2, z2, zl, sh_rkv, sh_rkv, sh_rkv, sh_l,
      mu_rkv, mu_rkv, mu_rkv, mu_l, w0, a0, k_k, k_a, r_k, gng, gnb, ww_pad, wa_pad)


def _rwkv_dec_pre_kernel(zr, zk, zv, zg, zl, pr, pk, pv, plr, mur, muk, muv, mul,
                         w0, a0, k_k, k_a, r_k, ww, wa,
                         r_o, w_o, k_o, v_o, al_o, be_o, bo_o, sg_o):
    seg = _seg_ones(zr.shape[1], HD_A)
    sh = lambda z, p, mu: z[...] + mu[...] * (p[...] - z[...])
    xr, xk, xv, xl = sh(zr, pr, mur), sh(zk, pk, muk), sh(zv, pv, muv), sh(zl, plr, mul)
    r, k2, v, logw, kk, a, bonus = _rwkv_vectors(
        xr, xk, xv, xl, w0[...], a0[...], k_k[...], k_a[...], r_k[...], ww[...], wa[...], seg)
    r_o[...] = r.T
    w_o[...] = jnp.exp(logw).T
    k_o[...] = k2.T
    v_o[...] = v.T
    al_o[...] = (-kk).T
    be_o[...] = (kk * a).T
    bo_o[...] = bonus
    sg_o[...] = _silu(zg[...])


def _rwkv_dec_pre(z2, zl, prev_rkv, prev_l, mu_rkv, mu_l, w0, a0, k_k, k_a, r_k, ww_pad, wa_pad):
    b, _ = z2.shape
    da = w0.shape[1]
    c = lambda w, j: pl.BlockSpec((b, w), lambda i, j=j: (0, j))
    in_specs = [c(da, 0), c(da, 1), c(da, 2), c(da, 3), c(LANES, 0),
                c(da, 0), c(da, 1), c(da, 2), c(LANES, 0),
                pl.BlockSpec((1, da), lambda i: (0, 0)), pl.BlockSpec((1, da), lambda i: (0, 1)),
                pl.BlockSpec((1, da), lambda i: (0, 2)), pl.BlockSpec((1, LANES), lambda i: (0, 0))]
    in_specs += [pl.BlockSpec((1, da), lambda i: (0, 0))] * 5
    in_specs += [pl.BlockSpec((LANES, da), lambda i: (0, 0))] * 2
    return pl.pallas_call(
        _rwkv_dec_pre_kernel,
        grid=(1,),
        in_specs=in_specs,
        out_specs=[pl.BlockSpec((da, b), lambda i: (0, 0))] * 6 + [pl.BlockSpec((b, da), lambda i: (0, 0))] * 2,
        out_shape=[jax.ShapeDtypeStruct((da, b), F32)] * 6 + [jax.ShapeDtypeStruct((b, da), F32)] * 2,
        compiler_params=_cparams(("arbitrary",)),
        name="rwkv_dec_pre",
    )(z2, z2, z2, z2, zl, prev_rkv, prev_rkv, prev_rkv, prev_l, mu_rkv, mu_rkv, mu_rkv, mu_l,
      w0, a0, k_k, k_a, r_k, ww_pad, wa_pad)


V_ROWS = 8


def _rwkv_dec_state_kernel(s_ref, w_ref, al_ref, be_ref, k_ref, r_ref, v_ref, so_ref, y_ref):
    hb, n, _, _ = s_ref.shape
    for h in range(hb):
        w, al, be, k2, r = w_ref[h], al_ref[h], be_ref[h], k_ref[h], r_ref[h]

        def v_rows(i, carry, h=h, w=w, al=al, be=be, k2=k2, r=r):
            sl = pl.ds(pl.multiple_of(i * V_ROWS, V_ROWS), V_ROWS)
            S = s_ref[h, sl]
            sa = jnp.sum(S * al[None], axis=1)
            vv = v_ref[h, sl, :]
            sn = S * w[None] + sa[:, None, :] * be[None] + vv[:, None, :] * k2[None]
            so_ref[h, sl] = sn
            y_ref[h, sl, :] = jnp.sum(sn * r[None], axis=1)
            return carry

        lax.fori_loop(0, n // V_ROWS, v_rows, 0)


def _rwkv_dec_state(s_hvkb, w, al, be, k2, r, v, hb):
    h, n, _, b = s_hvkb.shape
    sspec = pl.BlockSpec((hb, n, n, b), lambda i: (i, 0, 0, 0))
    rspec = pl.BlockSpec((hb, n, b), lambda i: (i, 0, 0))
    return pl.pallas_call(
        _rwkv_dec_state_kernel,
        grid=(h // hb,),
        in_specs=[sspec] + [rspec] * 6,
        out_specs=[sspec, rspec],
        out_shape=[jax.ShapeDtypeStruct(s_hvkb.shape, F32), jax.ShapeDtypeStruct((h, n, b), F32)],
        compiler_params=_cparams(("parallel",)),
        name="rwkv_dec_state",
    )(s_hvkb, w, al, be, k2, r, v)


def _rotate_pairs(x, cos, sin_signed):
    n = x.shape[-1]
    lane = lax.broadcasted_iota(jnp.int32, x.shape, x.ndim - 1)
    nxt = pltpu.roll(x, n - 1, x.ndim - 1)
    prv = pltpu.roll(x, 1, x.ndim - 1)
    swapped = jnp.where(lane % 2 == 0, nxt, prv)
    return x * cos + swapped * sin_signed


def _head_norm_b(y, g, b):
    mu = jnp.mean(y, axis=-1, keepdims=True)
    yc = y - mu
    var = jnp.mean(yc * yc, axis=-1, keepdims=True)
    return yc * lax.rsqrt(var + GN_EPS_B) * g + b


def _ret_prompt_kernel(lg_ref, q_ref, k_ref, v_ref, g_ref, cos_ref, sin_ref, gng, gnb,
                       y_ref, s_ref, S_scr, dm_scr):
    c = pl.program_id(1)
    nc = pl.num_programs(1)
    L = q_ref.shape[0]
    d = cos_ref.shape[1]
    nh = q_ref.shape[1] // d

    @pl.when(c == 0)
    def _():
        S_scr[...] = jnp.zeros_like(S_scr)
        ii = lax.broadcasted_iota(jnp.int32, (L, L), 0)
        jj = lax.broadcasted_iota(jnp.int32, (L, L), 1)
        diff = (ii - jj).astype(F32)
        for h in range(nh):
            lg = lg_ref[h, 0:1, 0:1]
            dm_scr[h] = jnp.where(diff >= 0, jnp.exp(jnp.maximum(diff, 0.0) * lg), 0.0)

    cos, sin = cos_ref[...], sin_ref[...]
    idx = lax.broadcasted_iota(jnp.int32, (L, 1), 0).astype(F32)

    def head(h):
        ln = slice(h * d, (h + 1) * d)
        lg = lg_ref[h, 0:1, 0:1]
        q = _rotate_pairs(q_ref[:, ln], cos, sin)
        k = _rotate_pairs(k_ref[:, ln], cos, sin) * (d ** -0.5)
        v = v_ref[:, ln]
        S = S_scr[h]
        yield
        scores = _bdot_nt(q, k) * dm_scr[h]
        cross = _bdot(q, S) * jnp.exp((idx + 1.0) * lg)
        yield
        inner = _bdot(scores, v)
        kdec = jnp.exp((L - 1.0 - idx) * lg)
        S_scr[h] = S * jnp.exp(L * lg) + _bdot_tn(k * kdec, v)
        yield
        y = _head_norm_b(inner + cross, gng[:, ln], gnb[:, ln]) * _silu(g_ref[:, ln])
        y_ref[:, ln] = y.astype(y_ref.dtype)

    active = [head(h) for h in range(nh)]
    while active:
        active = [gen for gen in active if next(gen, True) is None]

    @pl.when(c == nc - 1)
    def _():
        s_ref[...] = S_scr[...]


def _ret_prompt(z2, lg_tile, cos, sin, gng, gnb, col0):
    b, t, _ = z2.shape
    d = cos.shape[1]
    db = gng.shape[1]
    nh = db // d
    L = _pick_tile(t, (CHUNK_B,))
    nchunk = t // L
    c0 = col0 // db
    zspec = lambda off: pl.BlockSpec((None, L, db), lambda bi, ci, off=off: (bi, ci, c0 + off))
    tspec = pl.BlockSpec((L, d), lambda bi, ci: (ci, 0))
    vspec = pl.BlockSpec((1, db), lambda bi, ci: (0, 0))
    return pl.pallas_call(
        _ret_prompt_kernel,
        grid=(b, nchunk),
        in_specs=[pl.BlockSpec(lg_tile.shape, lambda bi, ci: (0, 0, 0)),
                  zspec(0), zspec(1), zspec(2), zspec(3), tspec, tspec, vspec, vspec],
        out_specs=[pl.BlockSpec((None, L, db), lambda bi, ci: (bi, ci, 0)),
                   pl.BlockSpec((None, nh, d, d), lambda bi, ci: (bi, 0, 0, 0))],
        out_shape=[jax.ShapeDtypeStruct((b, t, db), BF16),
                   jax.ShapeDtypeStruct((b, nh, d, d), F32)],
        scratch_shapes=[pltpu.VMEM((nh, d, d), F32), pltpu.VMEM((nh, L, L), F32)],
        compiler_params=_cparams(("parallel", "arbitrary")),
        name="ret_prompt",
    )(lg_tile, z2, z2, z2, z2, cos, sin, gng, gnb)


def _ret_dec_pre_kernel(zq, zk, zv, zg, cos_ref, sin_ref, q_o, k_o, in_o, sg_o):
    d = cos_ref.shape[1] // H_B
    seg = _seg_ones(zq.shape[1], d)
    cos, sin = cos_ref[...], sin_ref[...]
    q = _rotate_pairs(zq[...], cos, sin)
    k = _rotate_pairs(zk[...], cos, sin) * (d ** -0.5)
    q_o[...] = q
    k_o[...] = k
    in_o[...] = _dot_exact_rhs(q * k, seg) * zv[...]
    sg_o[...] = _silu(zg[...])


def _ret_dec_pre(z2, cos, sin, col0):
    b, _ = z2.shape
    db = cos.shape[1]
    c0 = col0 // db
    c = lambda j: pl.BlockSpec((b, db), lambda i, j=j: (0, c0 + j))
    t = pl.BlockSpec((1, db), lambda i: (0, 0))
    return pl.pallas_call(
        _ret_dec_pre_kernel,
        grid=(1,),
        in_specs=[c(0), c(1), c(2), c(3), t, t],
        out_specs=[pl.BlockSpec((b, db), lambda i: (0, 0))] * 4,
        out_shape=[jax.ShapeDtypeStruct((b, db), F32)] * 4,
        compiler_params=_cparams(("arbitrary",)),
        name="ret_dec_pre",
    )(z2, z2, z2, z2, cos, sin)


def _ret_dec_state_kernel(lg_ref, s_ref, q_ref, k_ref, v_ref, so_ref, cr_ref):
    bb, nh, d, _ = s_ref.shape
    rowid = lax.broadcasted_iota(jnp.int32, (bb, 1), 0)
    for h in range(nh):
        ln = slice(h * d, (h + 1) * d)
        gamma = jnp.exp(lg_ref[h, 0:1, 0:1])
        q, k, v = q_ref[:, ln], k_ref[:, ln], v_ref[:, ln]
        cross = jnp.zeros((bb, d), F32)
        for b in range(bb):
            sel = rowid == b
            S = s_ref[b, h]
            cross = cross + _bdot(jnp.where(sel, q, 0.0), S)
            so_ref[b, h] = S * gamma + _bdot_tn(jnp.where(sel, k, 0.0), v)
        cr_ref[:, ln] = cross * gamma


def _ret_dec_state(lg_tile, s0, q, k, v, bb):
    b, nh, d, _ = s0.shape
    sspec = pl.BlockSpec((bb, nh, d, d), lambda i: (i, 0, 0, 0))
    rspec = pl.BlockSpec((bb, nh * d), lambda i: (i, 0))
    return pl.pallas_call(
        _ret_dec_state_kernel,
        grid=(b // bb,),
        in_specs=[pl.BlockSpec(lg_tile.shape, lambda i: (0, 0, 0)), sspec, rspec, rspec, rspec],
        out_specs=[sspec, rspec],
        out_shape=[jax.ShapeDtypeStruct(s0.shape, F32), jax.ShapeDtypeStruct((b, nh * d), F32)],
        compiler_params=_cparams(("parallel",)),
        name="ret_dec_state",
    )(lg_tile, s0, q, k, v)


def _dec_post_kernel(ya_ref, bo_ref, sga_ref, gag, gab, in_ref, cr_ref, sgb_ref, gbg, gbb,
                     oa_ref, ob_ref):
    da = ya_ref.shape[0]
    db = in_ref.shape[1]
    d = db // H_B
    sega = _seg_ones(da, HD_A)
    segb = _seg_ones(db, d)
    y = ya_ref[...].T
    mu = _dot_exact_rhs(y, sega) * (1.0 / HD_A)
    yc = y - mu
    var = _dot_exact_rhs(yc * yc, sega) * (1.0 / HD_A)
    hn = yc * lax.rsqrt(var + GN_EPS_A) * gag[...] + gab[...]
    oa_ref[...] = ((hn + bo_ref[...]) * sga_ref[...]).astype(oa_ref.dtype)
    y = in_ref[...] + cr_ref[...]
    mu = _dot_exact_rhs(y, segb) * (1.0 / d)
    yc = y - mu
    var = _dot_exact_rhs(yc * yc, segb) * (1.0 / d)
    hn = yc * lax.rsqrt(var + GN_EPS_B) * gbg[...] + gbb[...]
    ob_ref[...] = (hn * sgb_ref[...]).astype(ob_ref.dtype)


def _dec_post(ya_t, bonus, sga, gag, gab, inner, cross, sgb, gbg, gbb):
    da, b = ya_t.shape
    db = inner.shape[1]
    ft = pl.BlockSpec((da, b), lambda i: (0, 0))
    fa = pl.BlockSpec((b, da), lambda i: (0, 0))
    fb = pl.BlockSpec((b, db), lambda i: (0, 0))
    va = pl.BlockSpec((1, da), lambda i: (0, 0))
    vb = pl.BlockSpec((1, db), lambda i: (0, 0))
    return pl.pallas_call(
        _dec_post_kernel,
        grid=(1,),
        in_specs=[ft, fa, fa, va, va, fb, fb, fb, vb, vb],
        out_specs=[fa, fb],
        out_shape=[jax.ShapeDtypeStruct((b, da), BF16), jax.ShapeDtypeStruct((b, db), BF16)],
        compiler_params=_cparams(("arbitrary",)),
        name="dec_post",
    )(ya_t, bonus, sga, gag, gab, inner, cross, sgb, gbg, gbb)


def _rot_tables(pos, d):
    angle = 1.0 / (ROPE_BASE ** jnp.linspace(0.0, 1.0, d // 2, dtype=F32))
    theta = pos[:, None] * angle[None, :]
    cos = jnp.repeat(jnp.cos(theta), 2, axis=-1)
    sin = jnp.repeat(jnp.sin(theta), 2, axis=-1)
    sign = jnp.tile(jnp.array([-1.0, 1.0], F32), d // 2)
    return cos, sin * sign


def _diag_blocks(s_bd):
    b, g = s_bd.shape[:2]
    s6 = s_bd.reshape(b, g, HEADS_PER_GROUP, HD_A, HEADS_PER_GROUP, HD_A)
    blocks = jnp.stack([s6[:, :, i, :, i, :] for i in range(HEADS_PER_GROUP)], axis=2)
    return blocks.reshape(b, g * HEADS_PER_GROUP, HD_A, HD_A)


def _pick_tile(n, pref):
    for t in pref:
        if n % t == 0:
            return t
    return n


def kernel(x_prompt, x_sample, p_prompt, p_sample, state_wkv, state_shift, state_ret, g_ln, w_in,
           mu_shift, w0, w_wB, a0, w_aB, k_k, k_a, r_k, gn_a_g, gn_a_b, gn_b_g, gn_b_b, w_out, w_ple,
           w_ple_gate, g_final):
    depth = g_ln.shape[0]
    assert depth == 1
    bp, tp, d_model = x_prompt.shape
    bs, ts, _ = x_sample.shape
    assert ts == 1
    da = w0.shape[1]
    db = gn_b_g.shape[1]
    hd_b = db // H_B
    n_rkv = 3 * da
    a_shift = n_rkv + 2 * LORA
    assert LORA * 2 == LANES and a_shift == mu_shift.shape[1]
    assert tp % CHUNK_A == 0 and da % GROUP == 0

    w = w_in[0]
    w_all = _wprep(w, n_rkv, a_shift)
    col_b = n_rkv + da
    mu = mu_shift[0]
    mu_rkv = mu[:n_rkv].reshape(1, n_rkv)
    mu_l = mu[n_rkv:].reshape(1, LANES)
    zpad = jnp.zeros((LORA, da), F32)
    ww_pad = jnp.concatenate([w_wB[0], zpad], axis=0).astype(BF16)
    wa_pad = jnp.concatenate([zpad, w_aB[0]], axis=0).astype(BF16)
    v1 = lambda a: a[0].reshape(1, -1)
    w0_, a0_, kk_, ka_, rk_ = v1(w0), v1(a0), v1(k_k), v1(k_a), v1(r_k)
    gag, gab, gbg, gbb = v1(gn_a_g), v1(gn_a_b), v1(gn_b_g), v1(gn_b_b)
    woa = w_out[0][:da].astype(BF16)
    wob = w_out[0][da:].astype(BF16)
    wg = w_ple_gate[0].astype(BF16)
    wp = w_ple[0].astype(BF16)
    lg = jnp.log(1.0 - jnp.exp2(-5.0 - jnp.arange(H_B, dtype=F32)))
    lg_tile = jnp.broadcast_to(lg[:, None, None], (H_B, 8, LANES))

    tn = _pick_tile(w_all.shape[1] - LANES, (1024, 512, 256, 128))

    xp = x_prompt.reshape(bp * tp, d_model)
    zp, zlp = _proj(xp, g_ln[0], w_all, LANES, _pick_tile(bp * tp, (1024, 512, 256, 128)), tn)
    zp3 = zp.reshape(bp, tp, -1)
    zlp3 = zlp.reshape(bp, tp, LANES)
    sh0 = jnp.zeros((bp, 1, a_shift), F32)
    tb = _pick_tile(tp, (256, 128, 64))
    ya_p, s_bd = _rwkv_prompt(zp3, zlp3, sh0[:, :, :n_rkv], sh0[:, :, n_rkv:], mu_rkv, mu_l, w0_, a0_, kk_,
                              ka_, rk_, gag, gab, ww_pad, wa_pad, tb, da // GROUP)
    cos_p, sin_p = _rot_tables(jnp.arange(tp, dtype=F32), hd_b)
    yb_p, ret_p = _ret_prompt(zp3, lg_tile, cos_p, sin_p, gbg, gbb, col_b)
    y_prompt = _out_block(xp, ya_p.reshape(bp * tp, da), yb_p.reshape(bp * tp, db),
                          p_prompt[0].reshape(bp * tp, -1), woa, wob, wg, wp, g_final,
                          _pick_tile(bp * tp, (256, 128))).reshape(bp, tp, d_model)
    wkv_p = _diag_blocks(s_bd)
    shift_p = jnp.concatenate([zp3[:, -1, :n_rkv], zlp3[:, -1]], axis=-1)

    xs = x_sample.reshape(bs, d_model)
    zs, zls = _proj(xs, g_ln[0], w_all, LANES, bs, tn)
    prev = state_shift[0]
    r, wdec, k2, v, al, be, bonus, sga = _rwkv_dec_pre(
        zs, zls, prev[:, :n_rkv], prev[:, n_rkv:], mu_rkv, mu_l, w0_, a0_, kk_, ka_, rk_, ww_pad, wa_pad)
    h_a = da // HD_A
    hs = lambda a: a.reshape(h_a, HD_A, bs)
    s_hvkb = jnp.transpose(state_wkv[0], (1, 2, 3, 0))
    s_new, y_t = _rwkv_dec_state(s_hvkb, hs(wdec), hs(al), hs(be), hs(k2), hs(r), hs(v),
                                 _pick_tile(h_a, (2, 1)))
    wkv_s = jnp.transpose(s_new, (3, 0, 1, 2))
    cos_s, sin_s = _rot_tables((PAST_LEN + jnp.arange(ts)).astype(F32), hd_b)
    cos_s = jnp.tile(cos_s, (1, H_B))
    sin_s = jnp.tile(sin_s, (1, H_B))
    q_s, k_s, inner, sgb = _ret_dec_pre(zs, cos_s, sin_s, col_b)
    ret_s, cross = _ret_dec_state(lg_tile, state_ret[0], q_s, k_s,
                                  zs[:, col_b + 2 * db:col_b + 3 * db], _pick_tile(bs, (8,)))
    ya_s, yb_s = _dec_post(y_t.reshape(da, bs), bonus, sga, gag, gab, inner, cross, sgb, gbg, gbb)
    y_sample = _out_block(xs, ya_s, yb_s, p_sample[0].reshape(bs, -1), woa, wob, wg, wp, g_final,
                          bs).reshape(bs, 1, d_model)
    shift_s = jnp.concatenate([zs[:, :n_rkv], zls], axis=-1)

    return (y_prompt, y_sample, wkv_p[None], shift_p[None], ret_p[None],
            wkv_s[None], shift_s[None], ret_s[None])
```

```python
import functools

import jax
import jax.numpy as jnp
from jax import lax
from jax.experimental import pallas as pl
from jax.experimental.pallas import tpu as pltpu

F32 = jnp.float32
BF16 = jnp.bfloat16

HD_A = 64
LORA = 64
H_B = 4
ROPE_BASE = 10000.0
RMS_EPS = 1e-6
GN_EPS_A = HD_A * 1e-5
GN_EPS_B = 1e-5
PAST_LEN = 16384
DECAY_SCALE = 0.6065306597126334

LANES = 128
GROUP = 256
HEADS_PER_GROUP = GROUP // HD_A
CHUNK_A = 64
STACK = HEADS_PER_GROUP * CHUNK_A
CHUNK_B = 256
VMEM_LIMIT = 48 * 1024 * 1024


def _cparams(sem):
    return pltpu.CompilerParams(dimension_semantics=sem, vmem_limit_bytes=VMEM_LIMIT)


def _bdot(a, b):
    return jnp.dot(a.astype(BF16), b.astype(BF16), preferred_element_type=F32)


def _bdot_nt(a, b):
    return lax.dot_general(a.astype(BF16), b.astype(BF16), (((1,), (1,)), ((), ())),
                           preferred_element_type=F32)


def _bdot_tn(a, b):
    return lax.dot_general(a.astype(BF16), b.astype(BF16), (((0,), (0,)), ((), ())),
                           preferred_element_type=F32)


def _split3(x):
    hi = x.astype(BF16)
    r1 = x - hi.astype(F32)
    mid = r1.astype(BF16)
    lo = (r1 - mid.astype(F32)).astype(BF16)
    return hi, mid, lo


def _dot_exact_rhs(x, m01):
    hi = x.astype(BF16)
    lo = (x - hi.astype(F32)).astype(BF16)
    d = lambda t: jnp.dot(t, m01, preferred_element_type=F32)
    return d(hi) + d(lo)


def _dot_exact_lhs(m01, x):
    hi, mid, lo = _split3(x)
    d = lambda t: jnp.dot(m01, t, preferred_element_type=F32)
    return d(hi) + d(mid) + d(lo)


def _segsum(x, seg01):
    n = seg01.shape[0]
    parts = [_dot_exact_rhs(x[:, i:i + n], seg01) for i in range(0, x.shape[1], n)]
    return parts[0] if len(parts) == 1 else jnp.concatenate(parts, axis=1)


def _seg_ones(n, seg):
    i = lax.broadcasted_iota(jnp.int32, (n, n), 0) // seg
    j = lax.broadcasted_iota(jnp.int32, (n, n), 1) // seg
    return jnp.where(i == j, 1.0, 0.0).astype(BF16)


def _sigmoid(x):
    return 1.0 / (1.0 + jnp.exp(-x))


def _silu(x):
    return x * _sigmoid(x)


def _proj_kernel(x_ref, g_ref, w_ref, wl_ref, o_ref, ol_ref, u_ref):
    @pl.when(pl.program_id(1) == 0)
    def _():
        x = x_ref[...]
        ms = jnp.mean(x * x, axis=-1, keepdims=True)
        u = (x * lax.rsqrt(ms + RMS_EPS) * g_ref[...]).astype(BF16)
        u_ref[...] = u
        ol_ref[...] = jnp.dot(u, wl_ref[...], preferred_element_type=F32)

    o_ref[...] = jnp.dot(u_ref[...], w_ref[...], preferred_element_type=F32)


def _wprep_kernel(lo, hi, w_ref, o_ref):
    n = w_ref.shape[1]
    o_ref[:, :lo] = w_ref[:, :lo].astype(BF16)
    o_ref[:, lo:n - (hi - lo)] = w_ref[:, hi:].astype(BF16)
    o_ref[:, n - (hi - lo):] = w_ref[:, lo:hi].astype(BF16)


def _wprep(w, lo, hi, tr):
    d, n = w.shape
    assert lo % LANES == 0 and hi % LANES == 0
    return pl.pallas_call(
        functools.partial(_wprep_kernel, lo, hi),
        grid=(d // tr,),
        in_specs=[pl.BlockSpec((tr, n), lambda i: (i, 0))],
        out_specs=pl.BlockSpec((tr, n), lambda i: (i, 0)),
        out_shape=jax.ShapeDtypeStruct((d, n), BF16),
        compiler_params=_cparams(("parallel",)),
        name="w_prep",
    )(w)


def _proj(x2d, g_ln, w_all, nl, tm, tn):
    m, d = x2d.shape
    nc = w_all.shape[1] - nl
    return pl.pallas_call(
        _proj_kernel,
        grid=(m // tm, nc // tn),
        in_specs=[pl.BlockSpec((tm, d), lambda i, j: (i, 0)),
                  pl.BlockSpec((1, d), lambda i, j: (0, 0)),
                  pl.BlockSpec((d, tn), lambda i, j: (0, j)),
                  pl.BlockSpec((d, nl), lambda i, j: (0, nc // nl))],
        out_specs=[pl.BlockSpec((tm, tn), lambda i, j: (i, j)),
                   pl.BlockSpec((tm, nl), lambda i, j: (i, 0))],
        out_shape=[jax.ShapeDtypeStruct((m, nc), F32), jax.ShapeDtypeStruct((m, nl), F32)],
        scratch_shapes=[pltpu.VMEM((tm, d), BF16)],
        compiler_params=_cparams(("parallel", "arbitrary")),
        name="in_proj",
    )(x2d, g_ln.reshape(1, d), w_all, w_all)


def _out_kernel(x_ref, ya_ref, yb_ref, p_ref, woa_ref, wob_ref, wg_ref, wp_ref, gf_ref, o_ref):
    h2 = (x_ref[...]
          + jnp.dot(ya_ref[...], woa_ref[...], preferred_element_type=F32)
          + jnp.dot(yb_ref[...], wob_ref[...], preferred_element_type=F32))
    gate = _sigmoid(jnp.dot(h2.astype(BF16), wg_ref[...], preferred_element_type=F32))
    ple = jnp.dot(p_ref[...].astype(BF16), wp_ref[...], preferred_element_type=F32)
    h3 = h2 + gate * ple
    ms = jnp.mean(h3 * h3, axis=-1, keepdims=True)
    o_ref[...] = h3 * lax.rsqrt(ms + RMS_EPS) * gf_ref[...]


def _out_block(x2d, ya, yb, p2d, woa, wob, wg, wp, g_final, tm):
    m, d = x2d.shape
    da, db, dp = ya.shape[1], yb.shape[1], p2d.shape[1]
    row = lambda w: pl.BlockSpec((tm, w), lambda i: (i, 0))
    whole = lambda a: pl.BlockSpec(a.shape, lambda i: (0, 0), pipeline_mode=pl.Buffered(1))
    gf = g_final.reshape(1, d)
    return pl.pallas_call(
        _out_kernel,
        grid=(m // tm,),
        in_specs=[row(d), row(da), row(db), row(dp), whole(woa), whole(wob), whole(wg), whole(wp),
                  whole(gf)],
        out_specs=row(d),
        out_shape=jax.ShapeDtypeStruct((m, d), F32),
        compiler_params=_cparams(("parallel",)),
        name="out_block",
    )(x2d, ya, yb, p2d, woa, wob, wg, wp, gf)


def _rwkv_vectors(xr, xk, xv, xl, w0, a0, k_k, k_a, r_k, ww_pad, wa_pad, seg):
    lw = _bdot(jnp.tanh(xl), ww_pad)
    la = _bdot(xl, wa_pad)
    logw = -DECAY_SCALE * _sigmoid(w0 + lw)
    a = _sigmoid(a0 + la)
    kk = xk * k_k
    kk = kk * jnp.minimum(lax.rsqrt(_segsum(kk * kk, seg)), 1e12)
    k2 = xk * (1.0 + (a - 1.0) * k_a)
    bonus = _segsum(xr * k2 * r_k, seg) * xv
    return xr, k2, xv, logw, kk, a, bonus


def _rwkv_post(y, bonus, ga, gn_g, gn_b, seg):
    inv = 1.0 / HD_A
    mu = _segsum(y, seg) * inv
    yc = y - mu
    var = _segsum(yc * yc, seg) * inv
    hn = yc * lax.rsqrt(var + GN_EPS_A) * gn_g + gn_b
    return (hn + bonus) * _silu(ga)


def _rwkv_prompt_kernel(zr, zk, zv, zg, zl, pr, pk, pv, plr, sr, sk, sv, slr,
                        mur, muk, muv, mul, w0, a0, k_k, k_a, r_k, gng, gnb, ww, wa,
                        y_ref, s_ref,
                        S_scr, r_s, k_s, v_s, lw_s, kk_s, a_s, y_s, cum_s):
    t = pl.program_id(2)
    nt = pl.num_programs(2)
    tb = zr.shape[0]
    gps = zr.shape[1] // GROUP
    nchunk = tb // CHUNK_A
    seg = _seg_ones(GROUP, HD_A)

    @pl.when(t == 0)
    def _():
        S_scr[...] = jnp.zeros_like(S_scr)

    row = lax.broadcasted_iota(jnp.int32, (tb, 1), 0)

    def shifted(z_ref, p_ref, s0_ref, mu_ref):
        x = z_ref[...]
        last = jnp.where(t == 0, s0_ref[...], p_ref[7:8, :])
        prev = jnp.where(row == 0, last, pltpu.roll(x, 1, 0))
        return x + mu_ref[...] * (prev - x)

    xr = shifted(zr, pr, sr, mur)
    xk = shifted(zk, pk, sk, muk)
    xv = shifted(zv, pv, sv, muv)
    xl = shifted(zl, plr, slr, mul)
    r, k2, v, logw, kk, a, bonus = _rwkv_vectors(
        xr, xk, xv, xl, w0[...], a0[...], k_k[...], k_a[...], r_k[...], ww[...], wa[...], seg)
    r_s[...] = r
    k_s[...] = k2
    v_s[...] = v
    lw_s[...] = logw
    kk_s[...] = kk
    a_s[...] = a
    bi = lax.broadcasted_iota(jnp.int32, (tb, tb), 0)
    bj = lax.broadcasted_iota(jnp.int32, (tb, tb), 1)
    tri_blk = jnp.where(((bi // CHUNK_A) == (bj // CHUNK_A)) & (bi >= bj), 1.0, 0.0).astype(BF16)
    cum_s[...] = _dot_exact_lhs(tri_blk, logw)

    lane = lax.broadcasted_iota(jnp.int32, (1, GROUP), 1) // HD_A
    masks = [jnp.where(lane == h, 1.0, 0.0) for h in range(HEADS_PER_GROUP)]
    si =lax.broadcasted_iota(jnp.int32, (STACK, STACK), 0)
    sj = lax.broadcasted_iota(jnp.int32, (STACK, STACK), 1)
    same = (si // CHUNK_A) == (sj // CHUNK_A)
    strict = same & ((si % CHUNK_A) > (sj % CHUNK_A))
    incl = same & ((si % CHUNK_A) >= (sj % CHUNK_A))

    def stack(x):
        return jnp.concatenate([x * m for m in masks], axis=0)

    def live_rows(x, s):
        return jnp.concatenate([x[h * CHUNK_A + s:(h + 1) * CHUNK_A] for h in range(HEADS_PER_GROUP)], axis=0)

    def merge_rows(base, upd, s, accumulate):
        n = CHUNK_A - s
        parts = []
        for h in range(HEADS_PER_GROUP):
            lo = h * CHUNK_A
            parts.append(jnp.zeros((s, upd.shape[1]), F32) if base is None else base[lo:lo + s])
            new = upd[h * n:(h + 1) * n]
            parts.append(new + base[lo + s:lo + CHUNK_A] if accumulate else new)
        return jnp.concatenate(parts, axis=0)

    def chunk_group(sl, g):
        ln = slice(g * GROUP, (g + 1) * GROUP)
        rc, kc, vc, lwc, kkc, ac = r_s[sl, ln], k_s[sl, ln], v_s[sl, ln], lw_s[sl, ln], kk_s[sl, ln], a_s[sl, ln]
        cum = cum_s[sl, ln]
        e_pos = jnp.exp(cum)
        e_neg = jnp.exp(-cum)
        cum_end = cum[CHUNK_A - 1:CHUNK_A, :]
        e_tail = jnp.exp(cum_end - cum)
        beta = kkc * ac
        a_st = stack(-kkc * jnp.exp(cum - lwc))
        r_st = stack(rc * e_pos)
        b_st = stack(beta * e_neg)
        k_st = stack(kc * e_neg)
        v_st = stack(vc)
        bt_st = stack(beta * e_tail)
        kt_st = stack(kc * e_tail)

        S = S_scr[g]
        yield
        n_ab = jnp.where(strict, _bdot_nt(a_st, b_st), 0.0)
        n_ak = jnp.where(strict, _bdot_nt(a_st, k_st), 0.0)
        yield
        u = _bdot_nt(a_st, S) + _bdot(n_ak, v_st)
        npow = n_ab
        steps = CHUNK_A.bit_length() - 1
        for i in range(steps):
            s = 1 << i
            yield
            if s % 8:
                u = u + _bdot(npow, u)
                npow = _bdot(npow, npow)
            else:
                u = merge_rows(u, _bdot(live_rows(npow, s), u), s, True)
                if i + 1 < steps:
                    npow = merge_rows(None, _bdot(live_rows(npow, 2 * s), npow), 2 * s, False)
        yield
        w_rb = jnp.where(incl, _bdot_nt(r_st, b_st), 0.0)
        w_rk = jnp.where(incl, _bdot_nt(r_st, k_st), 0.0)
        yield
        y_st = _bdot_nt(r_st, S) + _bdot(w_rb, u) + _bdot(w_rk, v_st)
        y = y_st[0:CHUNK_A]
        for h in range(1, HEADS_PER_GROUP):
            y = y + y_st[h * CHUNK_A:(h + 1) * CHUNK_A]
        y_s[sl, ln] = y
        yield
        S_scr[g] = S * jnp.exp(cum_end) + _bdot_tn(u, bt_st) + _bdot_tn(v_st, kt_st)

    def chunk(j, carry):
        sl = pl.ds(pl.multiple_of(j * CHUNK_A, CHUNK_A), CHUNK_A)
        active = [chunk_group(sl, g) for g in range(gps)]
        while active:
            active = [gen for gen in active if next(gen, True) is None]
        return carry

    for j in range(nchunk):
        chunk(j, 0)

    y_ref[...] = _rwkv_post(y_s[...], bonus, zg[...], gng[...], gnb[...], seg).astype(y_ref.dtype)

    @pl.when(t == nt - 1)
    def _():
        s_ref[...] = S_scr[...]


def _rwkv_prompt(z2, zl, sh_rkv, sh_l, mu_rkv, mu_l, w0, a0, k_k, k_a, r_k, gng, gnb, ww_pad, wa_pad, tb, gps):
    b, t, _ = z2.shape
    da = w0.shape[1]
    wd = gps * GROUP
    ng = da // wd
    nt = t // tb
    per8 = tb // 8

    def zspec(off):
        return pl.BlockSpec((None, tb, wd), lambda bi, g, ti, off=off: (bi, ti, off + g))

    def pspec(off):
        return pl.BlockSpec((None, 8, wd),
                            lambda bi, g, ti, off=off: (bi, jnp.maximum(ti * per8 - 1, 0), off + g))

    def sspec(off):
        return pl.BlockSpec((None, 1, wd), lambda bi, g, ti, off=off: (bi, 0, off + g))

    def vspec(off=0):
        return pl.BlockSpec((1, wd), lambda bi, g, ti, off=off: (0, off + g))

    zl_spec = pl.BlockSpec((None, tb, LANES), lambda bi, g, ti: (bi, ti, 0))
    pl_spec = pl.BlockSpec((None, 8, LANES),
                           lambda bi, g, ti: (bi, jnp.maximum(ti * per8 - 1, 0), 0))
    sl_spec = pl.BlockSpec((None, 1, LANES), lambda bi, g, ti: (bi, 0, 0))
    ml_spec = pl.BlockSpec((1, LANES), lambda bi, g, ti: (0, 0))
    w_spec = pl.BlockSpec((LANES, wd), lambda bi, g, ti: (0, g))

    in_specs = [zspec(0), zspec(ng), zspec(2 * ng), zspec(3 * ng), zl_spec,
                pspec(0), pspec(ng), pspec(2 * ng), pl_spec,
                sspec(0), sspec(ng), sspec(2 * ng), sl_spec,
                vspec(0), vspec(ng), vspec(2 * ng), ml_spec,
                vspec(), vspec(), vspec(), vspec(), vspec(), vspec(), vspec(), w_spec, w_spec]
    blk = lambda: pltpu.VMEM((tb, wd), F32)
    return pl.pallas_call(
        _rwkv_prompt_kernel,
        grid=(b, ng, nt),
        in_specs=in_specs,
        out_specs=[pl.BlockSpec((None, tb, wd), lambda bi, g, ti: (bi, ti, g)),
                   pl.BlockSpec((None, gps, GROUP, GROUP), lambda bi, g, ti: (bi, g, 0, 0))],
        out_shape=[jax.ShapeDtypeStruct((b, t, da), BF16),
                   jax.ShapeDtypeStruct((b, ng * gps, GROUP, GROUP), F32)],
        scratch_shapes=[pltpu.VMEM((gps, GROUP, GROUP), F32)] + [blk() for _ in range(8)],
        compiler_params=_cparams(("parallel", "parallel", "arbitrary")),
        name="rwkv_prompt",
    )(z2, z2, z2, z2, zl, z2, z2, z2, zl, sh_rkv, sh_rkv, sh_rkv, sh_l,
      mu_rkv, mu_rkv, mu_rkv, mu_l, w0, a0, k_k, k_a, r_k, gng, gnb, ww_pad, wa_pad)


def _rwkv_dec_pre_kernel(zr, zk, zv, zg, zl, pr, pk, pv, plr, mur, muk, muv, mul,
                         w0, a0, k_k, k_a, r_k, ww, wa,
                         r_o, w_o, k_o, v_o, al_o, be_o, bo_o, sg_o):
    seg = _seg_ones(zr.shape[1], HD_A)
    sh = lambda z, p, mu: z[...] + mu[...] * (p[...] - z[...])
    xr, xk, xv, xl = sh(zr, pr, mur), sh(zk, pk, muk), sh(zv, pv, muv), sh(zl, plr, mul)
    r, k2, v, logw, kk, a, bonus = _rwkv_vectors(
        xr, xk, xv, xl, w0[...], a0[...], k_k[...], k_a[...], r_k[...], ww[...], wa[...], seg)
    r_o[...] = r.T
    w_o[...] = jnp.exp(logw).T
    k_o[...] = k2.T
    v_o[...] = v.T
    al_o[...] = (-kk).T
    be_o[...] = (kk * a).T
    bo_o[...] = bonus
    sg_o[...] = _silu(zg[...])


def _rwkv_dec_pre(z2, zl, prev_rkv, prev_l, mu_rkv, mu_l, w0, a0, k_k, k_a, r_k, ww_pad, wa_pad):
    b, _ = z2.shape
    da = w0.shape[1]
    c = lambda w, j: pl.BlockSpec((b, w), lambda i, j=j: (0, j))
    in_specs = [c(da, 0), c(da, 1), c(da, 2), c(da, 3), c(LANES, 0),
                c(da, 0), c(da, 1), c(da, 2), c(LANES, 0),
                pl.BlockSpec((1, da), lambda i: (0, 0)), pl.BlockSpec((1, da), lambda i: (0, 1)),
                pl.BlockSpec((1, da), lambda i: (0, 2)), pl.BlockSpec((1, LANES), lambda i: (0, 0))]
    in_specs += [pl.BlockSpec((1, da), lambda i: (0, 0))] * 5
    in_specs += [pl.BlockSpec((LANES, da), lambda i: (0, 0))] * 2
    return pl.pallas_call(
        _rwkv_dec_pre_kernel,
        grid=(1,),
        in_specs=in_specs,
        out_specs=[pl.BlockSpec((da, b), lambda i: (0, 0))] * 6 + [pl.BlockSpec((b, da), lambda i: (0, 0))] * 2,
        out_shape=[jax.ShapeDtypeStruct((da, b), F32)] * 6 + [jax.ShapeDtypeStruct((b, da), F32)] * 2,
        compiler_params=_cparams(("arbitrary",)),
        name="rwkv_dec_pre",
    )(z2, z2, z2, z2, zl, prev_rkv, prev_rkv, prev_rkv, prev_l, mu_rkv, mu_rkv, mu_rkv, mu_l,
      w0, a0, k_k, k_a, r_k, ww_pad, wa_pad)


V_ROWS = 8


def _rwkv_dec_state_kernel(s_ref, w_ref, al_ref, be_ref, k_ref, r_ref, v_ref, so_ref, y_ref):
    hb, n, _, _ = s_ref.shape
    for h in range(hb):
        w, al, be, k2, r = w_ref[h], al_ref[h], be_ref[h], k_ref[h], r_ref[h]

        def v_rows(i, carry, h=h, w=w, al=al, be=be, k2=k2, r=r):
            sl = pl.ds(pl.multiple_of(i * V_ROWS, V_ROWS), V_ROWS)
            S = s_ref[h, sl]
            sa = jnp.sum(S * al[None], axis=1)
            vv = v_ref[h, sl, :]
            sn = S * w[None] + sa[:, None, :] * be[None] + vv[:, None, :] * k2[None]
            so_ref[h, sl] = sn
            y_ref[h, sl, :] = jnp.sum(sn * r[None], axis=1)
            return carry

        lax.fori_loop(0, n // V_ROWS, v_rows, 0)


def _rwkv_dec_state(s_hvkb, w, al, be, k2, r, v, hb):
    h, n, _, b = s_hvkb.shape
    sspec = pl.BlockSpec((hb, n, n, b), lambda i: (i, 0, 0, 0))
    rspec = pl.BlockSpec((hb, n, b), lambda i: (i, 0, 0))
    return pl.pallas_call(
        _rwkv_dec_state_kernel,
        grid=(h // hb,),
        in_specs=[sspec] + [rspec] * 6,
        out_specs=[sspec, rspec],
        out_shape=[jax.ShapeDtypeStruct(s_hvkb.shape, F32), jax.ShapeDtypeStruct((h, n, b), F32)],
        compiler_params=_cparams(("parallel",)),
        name="rwkv_dec_state",
    )(s_hvkb, w, al, be, k2, r, v)


def _rotate_pairs(x, cos, sin_signed):
    n = x.shape[-1]
    lane = lax.broadcasted_iota(jnp.int32, x.shape, x.ndim - 1)
    nxt = pltpu.roll(x, n - 1, x.ndim - 1)
    prv = pltpu.roll(x, 1, x.ndim - 1)
    swapped = jnp.where(lane % 2 == 0, nxt, prv)
    return x * cos + swapped * sin_signed


def _head_norm_b(y, g, b):
    mu = jnp.mean(y, axis=-1, keepdims=True)
    yc = y - mu
    var = jnp.mean(yc * yc, axis=-1, keepdims=True)
    return yc * lax.rsqrt(var + GN_EPS_B) * g + b


def _ret_prompt_kernel(lg_ref, q_ref, k_ref, v_ref, g_ref, cos_ref, sin_ref, gng, gnb,
                       y_ref, s_ref, S_scr, dm_scr):
    c = pl.program_id(1)
    nc = pl.num_programs(1)
    L = q_ref.shape[0]
    d = cos_ref.shape[1]
    nh = q_ref.shape[1] // d

    @pl.when(c == 0)
    def _():
        S_scr[...] = jnp.zeros_like(S_scr)
        ii = lax.broadcasted_iota(jnp.int32, (L, L), 0)
        jj = lax.broadcasted_iota(jnp.int32, (L, L), 1)
        diff = (ii - jj).astype(F32)
        for h in range(nh):
            lg = lg_ref[h, 0:1, 0:1]
            dm_scr[h] = jnp.where(diff >= 0, jnp.exp(jnp.maximum(diff, 0.0) * lg), 0.0)

    cos, sin = cos_ref[...], sin_ref[...]
    idx = lax.broadcasted_iota(jnp.int32, (L, 1), 0).astype(F32)

    def head(h):
        ln = slice(h * d, (h + 1) * d)
        lg = lg_ref[h, 0:1, 0:1]
        q = _rotate_pairs(q_ref[:, ln], cos, sin)
        k = _rotate_pairs(k_ref[:, ln], cos, sin) * (d ** -0.5)
        v = v_ref[:, ln]
        S = S_scr[h]
        yield
        scores = _bdot_nt(q, k) * dm_scr[h]
        cross = _bdot(q, S) * jnp.exp((idx + 1.0) * lg)
        yield
        inner = _bdot(scores, v)
        kdec = jnp.exp((L - 1.0 - idx) * lg)
        S_scr[h] = S * jnp.exp(L * lg) + _bdot_tn(k * kdec, v)
        yield
        y = _head_norm_b(inner + cross, gng[:, ln], gnb[:, ln]) * _silu(g_ref[:, ln])
        y_ref[:, ln] = y.astype(y_ref.dtype)

    active = [head(h) for h in range(nh)]
    while active:
        active = [gen for gen in active if next(gen, True) is None]

    @pl.when(c == nc - 1)
    def _():
        s_ref[...] = S_scr[...]


def _ret_prompt(z2, lg_tile, cos, sin, gng, gnb, col0):
    b, t, _ = z2.shape
    d = cos.shape[1]
    db = gng.shape[1]
    nh = db // d
    L = _pick_tile(t, (CHUNK_B,))
    nchunk = t // L
    c0 = col0 // db
    zspec = lambda off: pl.BlockSpec((None, L, db), lambda bi, ci, off=off: (bi, ci, c0 + off))
    tspec = pl.BlockSpec((L, d), lambda bi, ci: (ci, 0))
    vspec = pl.BlockSpec((1, db), lambda bi, ci: (0, 0))
    return pl.pallas_call(
        _ret_prompt_kernel,
        grid=(b, nchunk),
        in_specs=[pl.BlockSpec(lg_tile.shape, lambda bi, ci: (0, 0, 0)),
                  zspec(0), zspec(1), zspec(2), zspec(3), tspec, tspec, vspec, vspec],
        out_specs=[pl.BlockSpec((None, L, db), lambda bi, ci: (bi, ci, 0)),
                   pl.BlockSpec((None, nh, d, d), lambda bi, ci: (bi, 0, 0, 0))],
        out_shape=[jax.ShapeDtypeStruct((b, t, db), BF16),
                   jax.ShapeDtypeStruct((b, nh, d, d), F32)],
        scratch_shapes=[pltpu.VMEM((nh, d, d), F32), pltpu.VMEM((nh, L, L), F32)],
        compiler_params=_cparams(("parallel", "arbitrary")),
        name="ret_prompt",
    )(lg_tile, z2, z2, z2, z2, cos, sin, gng, gnb)


def _ret_dec_pre_kernel(zq, zk, zv, zg, cos_ref, sin_ref, q_o, k_o, in_o, sg_o):
    d = cos_ref.shape[1] // H_B
    seg = _seg_ones(zq.shape[1], d)
    cos, sin = cos_ref[...], sin_ref[...]
    q = _rotate_pairs(zq[...], cos, sin)
    k = _rotate_pairs(zk[...], cos, sin) * (d ** -0.5)
    q_o[...] = q
    k_o[...] = k
    in_o[...] = _dot_exact_rhs(q * k, seg) * zv[...]
    sg_o[...] = _silu(zg[...])


def _ret_dec_pre(z2, cos, sin, col0):
    b, _ = z2.shape
    db = cos.shape[1]
    c0 = col0 // db
    c = lambda j: pl.BlockSpec((b, db), lambda i, j=j: (0, c0 + j))
    t = pl.BlockSpec((1, db), lambda i: (0, 0))
    return pl.pallas_call(
        _ret_dec_pre_kernel,
        grid=(1,),
        in_specs=[c(0), c(1), c(2), c(3), t, t],
        out_specs=[pl.BlockSpec((b, db), lambda i: (0, 0))] * 4,
        out_shape=[jax.ShapeDtypeStruct((b, db), F32)] * 4,
        compiler_params=_cparams(("arbitrary",)),
        name="ret_dec_pre",
    )(z2, z2, z2, z2, cos, sin)


def _ret_dec_state_kernel(lg_ref, s_ref, q_ref, k_ref, v_ref, so_ref, cr_ref):
    bb, nh, d, _ = s_ref.shape
    rowid = lax.broadcasted_iota(jnp.int32, (bb, 1), 0)
    for h in range(nh):
        ln = slice(h * d, (h + 1) * d)
        gamma = jnp.exp(lg_ref[h, 0:1, 0:1])
        q, k, v = q_ref[:, ln], k_ref[:, ln], v_ref[:, ln]
        cross = jnp.zeros((bb, d), F32)
        for b in range(bb):
            sel = rowid == b
            S = s_ref[b, h]
            cross = cross + _bdot(jnp.where(sel, q, 0.0), S)
            so_ref[b, h] = S * gamma + _bdot_tn(jnp.where(sel, k, 0.0), v)
        cr_ref[:, ln] = cross * gamma


def _ret_dec_state(lg_tile, s0, q, k, v, bb):
    b, nh, d, _ = s0.shape
    sspec = pl.BlockSpec((bb, nh, d, d), lambda i: (i, 0, 0, 0))
    rspec = pl.BlockSpec((bb, nh * d), lambda i: (i, 0))
    return pl.pallas_call(
        _ret_dec_state_kernel,
        grid=(b // bb,),
        in_specs=[pl.BlockSpec(lg_tile.shape, lambda i: (0, 0, 0)), sspec, rspec, rspec, rspec],
        out_specs=[sspec, rspec],
        out_shape=[jax.ShapeDtypeStruct(s0.shape, F32), jax.ShapeDtypeStruct((b, nh * d), F32)],
        compiler_params=_cparams(("parallel",)),
        name="ret_dec_state",
    )(lg_tile, s0, q, k, v)


def _dec_post_kernel(ya_ref, bo_ref, sga_ref, gag, gab, in_ref, cr_ref, sgb_ref, gbg, gbb,
                     oa_ref, ob_ref):
    da = ya_ref.shape[0]
    db = in_ref.shape[1]
    d = db // H_B
    sega = _seg_ones(da, HD_A)
    segb = _seg_ones(db, d)
    y = ya_ref[...].T
    mu = _dot_exact_rhs(y, sega) * (1.0 / HD_A)
    yc = y - mu
    var = _dot_exact_rhs(yc * yc, sega) * (1.0 / HD_A)
    hn = yc * lax.rsqrt(var + GN_EPS_A) * gag[...] + gab[...]
    oa_ref[...] = ((hn + bo_ref[...]) * sga_ref[...]).astype(oa_ref.dtype)
    y = in_ref[...] + cr_ref[...]
    mu = _dot_exact_rhs(y, segb) * (1.0 / d)
    yc = y - mu
    var = _dot_exact_rhs(yc * yc, segb) * (1.0 / d)
    hn = yc * lax.rsqrt(var + GN_EPS_B) * gbg[...] + gbb[...]
    ob_ref[...] = (hn * sgb_ref[...]).astype(ob_ref.dtype)


def _dec_post(ya_t, bonus, sga, gag, gab, inner, cross, sgb, gbg, gbb):
    da, b = ya_t.shape
    db = inner.shape[1]
    ft = pl.BlockSpec((da, b), lambda i: (0, 0))
    fa = pl.BlockSpec((b, da), lambda i: (0, 0))
    fb = pl.BlockSpec((b, db), lambda i: (0, 0))
    va = pl.BlockSpec((1, da), lambda i: (0, 0))
    vb = pl.BlockSpec((1, db), lambda i: (0, 0))
    return pl.pallas_call(
        _dec_post_kernel,
        grid=(1,),
        in_specs=[ft, fa, fa, va, va, fb, fb, fb, vb, vb],
        out_specs=[fa, fb],
        out_shape=[jax.ShapeDtypeStruct((b, da), BF16), jax.ShapeDtypeStruct((b, db), BF16)],
        compiler_params=_cparams(("arbitrary",)),
        name="dec_post",
    )(ya_t, bonus, sga, gag, gab, inner, cross, sgb, gbg, gbb)


def _rot_tables(pos, d):
    angle = 1.0 / (ROPE_BASE ** jnp.linspace(0.0, 1.0, d // 2, dtype=F32))
    theta = pos[:, None] * angle[None, :]
    cos = jnp.repeat(jnp.cos(theta), 2, axis=-1)
    sin = jnp.repeat(jnp.sin(theta), 2, axis=-1)
    sign = jnp.tile(jnp.array([-1.0, 1.0], F32), d // 2)
    return cos, sin * sign


def _diag_blocks(s_bd):
    b, g = s_bd.shape[:2]
    s6 = s_bd.reshape(b, g, HEADS_PER_GROUP, HD_A, HEADS_PER_GROUP, HD_A)
    blocks = jnp.stack([s6[:, :, i, :, i, :] for i in range(HEADS_PER_GROUP)], axis=2)
    return blocks.reshape(b, g * HEADS_PER_GROUP, HD_A, HD_A)


def _pick_tile(n, pref):
    for t in pref:
        if n % t == 0:
            return t
    return n


def kernel(x_prompt, x_sample, p_prompt, p_sample, state_wkv, state_shift, state_ret, g_ln, w_in,
           mu_shift, w0, w_wB, a0, w_aB, k_k, k_a, r_k, gn_a_g, gn_a_b, gn_b_g, gn_b_b, w_out, w_ple,
           w_ple_gate, g_final):
    depth = g_ln.shape[0]
    assert depth == 1
    bp, tp, d_model = x_prompt.shape
    bs, ts, _ = x_sample.shape
    assert ts == 1
    da = w0.shape[1]
    db = gn_b_g.shape[1]
    hd_b = db // H_B
    n_rkv = 3 * da
    a_shift = n_rkv + 2 * LORA
    assert LORA * 2 == LANES and a_shift == mu_shift.shape[1]
    assert tp % CHUNK_A == 0 and da % GROUP == 0

    w = w_in[0]
    w_all = _wprep(w, n_rkv, a_shift, _pick_tile(d_model, (256, 128)))
    col_b = n_rkv + da
    mu = mu_shift[0]
    mu_rkv = mu[:n_rkv].reshape(1, n_rkv)
    mu_l = mu[n_rkv:].reshape(1, LANES)
    zpad = jnp.zeros((LORA, da), F32)
    ww_pad = jnp.concatenate([w_wB[0], zpad], axis=0).astype(BF16)
    wa_pad = jnp.concatenate([zpad, w_aB[0]], axis=0).astype(BF16)
    v1 = lambda a: a[0].reshape(1, -1)
    w0_, a0_, kk_, ka_, rk_ = v1(w0), v1(a0), v1(k_k), v1(k_a), v1(r_k)
    gag, gab, gbg, gbb = v1(gn_a_g), v1(gn_a_b), v1(gn_b_g), v1(gn_b_b)
    woa = w_out[0][:da].astype(BF16)
    wob = w_out[0][da:].astype(BF16)
    wg = w_ple_gate[0].astype(BF16)
    wp = w_ple[0].astype(BF16)
    lg = jnp.log(1.0 - jnp.exp2(-5.0 - jnp.arange(H_B, dtype=F32)))
    lg_tile = jnp.broadcast_to(lg[:, None, None], (H_B, 8, LANES))

    tn = _pick_tile(w_all.shape[1] - LANES, (1024, 512, 256, 128))

    xp = x_prompt.reshape(bp * tp, d_model)
    zp, zlp = _proj(xp, g_ln[0], w_all, LANES, _pick_tile(bp * tp, (1024, 512, 256, 128)), tn)
    zp3 = zp.reshape(bp, tp, -1)
    zlp3 = zlp.reshape(bp, tp, LANES)
    sh0 = jnp.zeros((bp, 1, a_shift), F32)
    tb = _pick_tile(tp, (256, 128, 64))
    ya_p, s_bd = _rwkv_prompt(zp3, zlp3, sh0[:, :, :n_rkv], sh0[:, :, n_rkv:], mu_rkv, mu_l, w0_, a0_, kk_,
                              ka_, rk_, gag, gab, ww_pad, wa_pad, tb, da // GROUP)
    cos_p, sin_p = _rot_tables(jnp.arange(tp, dtype=F32), hd_b)
    yb_p, ret_p = _ret_prompt(zp3, lg_tile, cos_p, sin_p, gbg, gbb, col_b)
    y_prompt = _out_block(xp, ya_p.reshape(bp * tp, da), yb_p.reshape(bp * tp, db),
                          p_prompt[0].reshape(bp * tp, -1), woa, wob, wg, wp, g_final,
                          _pick_tile(bp * tp, (256, 128))).reshape(bp, tp, d_model)
    wkv_p = _diag_blocks(s_bd)
    shift_p = jnp.concatenate([zp3[:, -1, :n_rkv], zlp3[:, -1]], axis=-1)

    xs = x_sample.reshape(bs, d_model)
    zs, zls = _proj(xs, g_ln[0], w_all, LANES, bs, tn)
    prev = state_shift[0]
    r, wdec, k2, v, al, be, bonus, sga = _rwkv_dec_pre(
        zs, zls, prev[:, :n_rkv], prev[:, n_rkv:], mu_rkv, mu_l, w0_, a0_, kk_, ka_, rk_, ww_pad, wa_pad)
    h_a = da // HD_A
    hs = lambda a: a.reshape(h_a, HD_A, bs)
    s_hvkb = jnp.transpose(state_wkv[0], (1, 2, 3, 0))
    s_new, y_t = _rwkv_dec_state(s_hvkb, hs(wdec), hs(al), hs(be), hs(k2), hs(r), hs(v),
                                 _pick_tile(h_a, (2, 1)))
    wkv_s = jnp.transpose(s_new, (3, 0, 1, 2))
    cos_s, sin_s = _rot_tables((PAST_LEN + jnp.arange(ts)).astype(F32), hd_b)
    cos_s = jnp.tile(cos_s, (1, H_B))
    sin_s = jnp.tile(sin_s, (1, H_B))
    q_s, k_s, inner, sgb = _ret_dec_pre(zs, cos_s, sin_s, col_b)
    ret_s, cross = _ret_dec_state(lg_tile, state_ret[0], q_s, k_s,
                                  zs[:, col_b + 2 * db:col_b + 3 * db], _pick_tile(bs, (8,)))
    ya_s, yb_s = _dec_post(y_t.reshape(da, bs), bonus, sga, gag, gab, inner, cross, sgb, gbg, gbb)
    y_sample = _out_block(xs, ya_s, yb_s, p_sample[0].reshape(bs, -1), woa, wob, wg, wp, g_final,
                          bs).reshape(bs, 1, d_model)
    shift_s = jnp.concatenate([zs[:, :n_rkv], zls], axis=-1)

    return (y_prompt, y_sample, wkv_p[None], shift_p[None], ret_p[None],
            wkv_s[None], shift_s[None], ret_s[None])
```

```python
import functools

import jax
import jax.numpy as jnp
from jax import lax
from jax.experimental import pallas as pl
from jax.experimental.pallas import tpu as pltpu

F32 = jnp.float32
BF16 = jnp.bfloat16

HD_A = 64
LORA = 64
H_B = 4
ROPE_BASE = 10000.0
RMS_EPS = 1e-6
GN_EPS_A = HD_A * 1e-5
GN_EPS_B = 1e-5
PAST_LEN = 16384
DECAY_SCALE = 0.6065306597126334

LANES = 128
GROUP = 256
HEADS_PER_GROUP = GROUP // HD_A
CHUNK_A = 64
STACK = HEADS_PER_GROUP * CHUNK_A
CHUNK_B = 256
VMEM_LIMIT = 48 * 1024 * 1024


def _cparams(sem):
    return pltpu.CompilerParams(dimension_semantics=sem, vmem_limit_bytes=VMEM_LIMIT)


def _bdot(a, b):
    return jnp.dot(a.astype(BF16), b.astype(BF16), preferred_element_type=F32)


def _bdot_nt(a, b):
    return lax.dot_general(a.astype(BF16), b.astype(BF16), (((1,), (1,)), ((), ())),
                           preferred_element_type=F32)


def _bdot_tn(a, b):
    return lax.dot_general(a.astype(BF16), b.astype(BF16), (((0,), (0,)), ((), ())),
                           preferred_element_type=F32)


def _split3(x):
    hi = x.astype(BF16)
    r1 = x - hi.astype(F32)
    mid = r1.astype(BF16)
    lo = (r1 - mid.astype(F32)).astype(BF16)
    return hi, mid, lo


def _dot_exact_rhs(x, m01):
    hi = x.astype(BF16)
    lo = (x - hi.astype(F32)).astype(BF16)
    d = lambda t: jnp.dot(t, m01, preferred_element_type=F32)
    return d(hi) + d(lo)


def _dot_exact_lhs(m01, x):
    hi, mid, lo = _split3(x)
    d = lambda t: jnp.dot(m01, t, preferred_element_type=F32)
    return d(hi) + d(mid) + d(lo)


def _segsum(x, seg01):
    n = seg01.shape[0]
    parts = [_dot_exact_rhs(x[:, i:i + n], seg01) for i in range(0, x.shape[1], n)]
    return parts[0] if len(parts) == 1 else jnp.concatenate(parts, axis=1)


def _seg_ones(n, seg):
    i = lax.broadcasted_iota(jnp.int32, (n, n), 0) // seg
    j = lax.broadcasted_iota(jnp.int32, (n, n), 1) // seg
    return jnp.where(i == j, 1.0, 0.0).astype(BF16)


def _sigmoid(x):
    return 1.0 / (1.0 + jnp.exp(-x))


def _silu(x):
    return x * _sigmoid(x)


def _proj_kernel(x_ref, g_ref, w_ref, wl_ref, o_ref, ol_ref, u_ref):
    @pl.when(pl.program_id(1) == 0)
    def _():
        x = x_ref[...]
        ms = jnp.mean(x * x, axis=-1, keepdims=True)
        u = (x * lax.rsqrt(ms + RMS_EPS) * g_ref[...]).astype(BF16)
        u_ref[...] = u
        ol_ref[...] = jnp.dot(u, wl_ref[...], preferred_element_type=F32)

    o_ref[...] = jnp.dot(u_ref[...], w_ref[...], preferred_element_type=F32)


def _wprep_kernel(lo, hi, w_ref, o_ref):
    n = w_ref.shape[1]
    o_ref[:, :lo] = w_ref[:, :lo].astype(BF16)
    o_ref[:, lo:n - (hi - lo)] = w_ref[:, hi:].astype(BF16)
    o_ref[:, n - (hi - lo):] = w_ref[:, lo:hi].astype(BF16)


def _wprep(w, lo, hi, tr):
    d, n = w.shape
    assert lo % LANES == 0 and hi % LANES == 0
    return pl.pallas_call(
        functools.partial(_wprep_kernel, lo, hi),
        grid=(d // tr,),
        in_specs=[pl.BlockSpec((tr, n), lambda i: (i, 0))],
        out_specs=pl.BlockSpec((tr, n), lambda i: (i, 0)),
        out_shape=jax.ShapeDtypeStruct((d, n), BF16),
        compiler_params=_cparams(("parallel",)),
        name="w_prep",
    )(w)


def _proj(x2d, g_ln, w_all, nl, tm, tn):
    m, d = x2d.shape
    nc = w_all.shape[1] - nl
    return pl.pallas_call(
        _proj_kernel,
        grid=(m // tm, nc // tn),
        in_specs=[pl.BlockSpec((tm, d), lambda i, j: (i, 0)),
                  pl.BlockSpec((1, d), lambda i, j: (0, 0)),
                  pl.BlockSpec((d, tn), lambda i, j: (0, j)),
                  pl.BlockSpec((d, nl), lambda i, j: (0, nc // nl))],
        out_specs=[pl.BlockSpec((tm, tn), lambda i, j: (i, j)),
                   pl.BlockSpec((tm, nl), lambda i, j: (i, 0))],
        out_shape=[jax.ShapeDtypeStruct((m, nc), F32), jax.ShapeDtypeStruct((m, nl), F32)],
        scratch_shapes=[pltpu.VMEM((tm, d), BF16)],
        compiler_params=_cparams(("parallel", "arbitrary")),
        name="in_proj",
    )(x2d, g_ln.reshape(1, d), w_all, w_all)


def _out_kernel(x_ref, ya_ref, yb_ref, p_ref, woa_ref, wob_ref, wg_ref, wp_ref, gf_ref, o_ref):
    h2 = (x_ref[...]
          + jnp.dot(ya_ref[...], woa_ref[...], preferred_element_type=F32)
          + jnp.dot(yb_ref[...], wob_ref[...], preferred_element_type=F32))
    gate = _sigmoid(jnp.dot(h2.astype(BF16), wg_ref[...], preferred_element_type=F32))
    ple = jnp.dot(p_ref[...].astype(BF16), wp_ref[...], preferred_element_type=F32)
    h3 = h2 + gate * ple
    ms = jnp.mean(h3 * h3, axis=-1, keepdims=True)
    o_ref[...] = h3 * lax.rsqrt(ms + RMS_EPS) * gf_ref[...]


def _out_block(x2d, ya, yb, p2d, woa, wob, wg, wp, g_final, tm):
    m, d = x2d.shape
    da, db, dp = ya.shape[1], yb.shape[1], p2d.shape[1]
    row = lambda w: pl.BlockSpec((tm, w), lambda i: (i, 0))
    whole = lambda a: pl.BlockSpec(a.shape, lambda i: (0, 0), pipeline_mode=pl.Buffered(1))
    gf = g_final.reshape(1, d)
    return pl.pallas_call(
        _out_kernel,
        grid=(m // tm,),
        in_specs=[row(d), row(da), row(db), row(dp), whole(woa), whole(wob), whole(wg), whole(wp),
                  whole(gf)],
        out_specs=row(d),
        out_shape=jax.ShapeDtypeStruct((m, d), F32),
        compiler_params=_cparams(("parallel",)),
        name="out_block",
    )(x2d, ya, yb, p2d, woa, wob, wg, wp, gf)


def _rwkv_vectors(xr, xk, xv, xl, w0, a0, k_k, k_a, r_k, ww_pad, wa_pad, seg):
    lw = _bdot(jnp.tanh(xl), ww_pad)
    la = _bdot(xl, wa_pad)
    logw = -DECAY_SCALE * _sigmoid(w0 + lw)
    a = _sigmoid(a0 + la)
    kk = xk * k_k
    kk = kk * jnp.minimum(lax.rsqrt(_segsum(kk * kk, seg)), 1e12)
    k2 = xk * (1.0 + (a - 1.0) * k_a)
    bonus = _segsum(xr * k2 * r_k, seg) * xv
    return xr, k2, xv, logw, kk, a, bonus


def _rwkv_post(y, bonus, ga, gn_g, gn_b, seg):
    inv = 1.0 / HD_A
    mu = _segsum(y, seg) * inv
    yc = y - mu
    var = _segsum(yc * yc, seg) * inv
    hn = yc * lax.rsqrt(var + GN_EPS_A) * gn_g + gn_b
    return (hn + bonus) * _silu(ga)


def _rwkv_prompt_kernel(zr, zk, zv, zg, zl, pr, pk, pv, plr, sr, sk, sv, slr,
                        mur, muk, muv, mul, w0, a0, k_k, k_a, r_k, gng, gnb, ww, wa,
                        y_ref, s_ref,
                        S_scr, r_s, k_s, v_s, lw_s, kk_s, a_s, y_s, cum_s):
    t = pl.program_id(2)
    nt = pl.num_programs(2)
    tb = zr.shape[0]
    gps = zr.shape[1] // GROUP
    nchunk = tb // CHUNK_A
    seg = _seg_ones(GROUP, HD_A)

    @pl.when(t == 0)
    def _():
        S_scr[...] = jnp.zeros_like(S_scr)

    row = lax.broadcasted_iota(jnp.int32, (tb, 1), 0)

    def shifted(z_ref, p_ref, s0_ref, mu_ref):
        x = z_ref[...]
        last = jnp.where(t == 0, s0_ref[...], p_ref[7:8, :])
        prev = jnp.where(row == 0, last, pltpu.roll(x, 1, 0))
        return x + mu_ref[...] * (prev - x)

    xr = shifted(zr, pr, sr, mur)
    xk = shifted(zk, pk, sk, muk)
    xv = shifted(zv, pv, sv, muv)
    xl = shifted(zl, plr, slr, mul)
    r, k2, v, logw, kk, a, bonus = _rwkv_vectors(
        xr, xk, xv, xl, w0[...], a0[...], k_k[...], k_a[...], r_k[...], ww[...], wa[...], seg)
    r_s[...] = r
    k_s[...] = k2
    v_s[...] = v
    lw_s[...] = logw
    kk_s[...] = kk
    a_s[...] = a
    bi = lax.broadcasted_iota(jnp.int32, (tb, tb), 0)
    bj = lax.broadcasted_iota(jnp.int32, (tb, tb), 1)
    tri_blk = jnp.where(((bi // CHUNK_A) == (bj // CHUNK_A)) & (bi >= bj), 1.0, 0.0).astype(BF16)
    cum_s[...] = _dot_exact_lhs(tri_blk, logw)

    lane = lax.broadcasted_iota(jnp.int32, (1, GROUP), 1) // HD_A
    masks = [jnp.where(lane == h, 1.0, 0.0) for h in range(HEADS_PER_GROUP)]
    si =lax.broadcasted_iota(jnp.int32, (STACK, STACK), 0)
    sj = lax.broadcasted_iota(jnp.int32, (STACK, STACK), 1)
    same = (si // CHUNK_A) == (sj // CHUNK_A)
    strict = same & ((si % CHUNK_A) > (sj % CHUNK_A))
    incl = same & ((si % CHUNK_A) >= (sj % CHUNK_A))

    def stack(x):
        return jnp.concatenate([x * m for m in masks], axis=0)

    def live_rows(x, s):
        return jnp.concatenate([x[h * CHUNK_A + s:(h + 1) * CHUNK_A] for h in range(HEADS_PER_GROUP)], axis=0)

    def merge_rows(base, upd, s, accumulate):
        n = CHUNK_A - s
        parts = []
        for h in range(HEADS_PER_GROUP):
            lo = h * CHUNK_A
            parts.append(jnp.zeros((s, upd.shape[1]), F32) if base is None else base[lo:lo + s])
            new = upd[h * n:(h + 1) * n]
            parts.append(new + base[lo + s:lo + CHUNK_A] if accumulate else new)
        return jnp.concatenate(parts, axis=0)

    def chunk_group(sl, g):
        ln = slice(g * GROUP, (g + 1) * GROUP)
        rc, kc, vc, lwc, kkc, ac = r_s[sl, ln], k_s[sl, ln], v_s[sl, ln], lw_s[sl, ln], kk_s[sl, ln], a_s[sl, ln]
        cum = cum_s[sl, ln]
        e_pos = jnp.exp(cum)
        e_neg = jnp.exp(-cum)
        cum_end = cum[CHUNK_A - 1:CHUNK_A, :]
        e_tail = jnp.exp(cum_end - cum)
        beta = kkc * ac
        a_st = stack(-kkc * jnp.exp(cum - lwc))
        r_st = stack(rc * e_pos)
        b_st = stack(beta * e_neg)
        k_st = stack(kc * e_neg)
        v_st = stack(vc)
        bt_st = stack(beta * e_tail)
        kt_st = stack(kc * e_tail)

        S = S_scr[g]
        yield
        n_ab = jnp.where(strict, _bdot_nt(a_st, b_st), 0.0)
        n_ak = jnp.where(strict, _bdot_nt(a_st, k_st), 0.0)
        yield
        u = _bdot_nt(a_st, S) + _bdot(n_ak, v_st)
        npow = n_ab
        steps = CHUNK_A.bit_length() - 1
        for i in range(steps):
            s = 1 << i
            yield
            if s % 8:
                u = u + _bdot(npow, u)
                npow = _bdot(npow, npow)
            else:
                u = merge_rows(u, _bdot(live_rows(npow, s), u), s, True)
                if i + 1 < steps:
                    npow = merge_rows(None, _bdot(live_rows(npow, 2 * s), npow), 2 * s, False)
        yield
        w_rb = jnp.where(incl, _bdot_nt(r_st, b_st), 0.0)
        w_rk = jnp.where(incl, _bdot_nt(r_st, k_st), 0.0)
        yield
        y_st = _bdot_nt(r_st, S) + _bdot(w_rb, u) + _bdot(w_rk, v_st)
        y = y_st[0:CHUNK_A]
        for h in range(1, HEADS_PER_GROUP):
            y = y + y_st[h * CHUNK_A:(h + 1) * CHUNK_A]
        y_s[sl, ln] = y
        yield
        S_scr[g] = S * jnp.exp(cum_end) + _bdot_tn(u, bt_st) + _bdot_tn(v_st, kt_st)

    def chunk(j, carry):
        sl = pl.ds(pl.multiple_of(j * CHUNK_A, CHUNK_A), CHUNK_A)
        active = [chunk_group(sl, g) for g in range(gps)]
        while active:
            active = [gen for gen in active if next(gen, True) is None]
        return carry

    for j in range(nchunk):
        chunk(j, 0)

    y_ref[...] = _rwkv_post(y_s[...], bonus, zg[...], gng[...], gnb[...], seg).astype(y_ref.dtype)

    @pl.when(t == nt - 1)
    def _():
        for g in range(gps):
            for h in range(HEADS_PER_GROUP):
                sl = slice(h * HD_A, (h + 1) * HD_A)
                s_ref[g * HEADS_PER_GROUP + h] = S_scr[g, sl, sl]


def _rwkv_prompt(z2, zl, sh_rkv, sh_l, mu_rkv, mu_l, w0, a0, k_k, k_a, r_k, gng, gnb, ww_pad, wa_pad, tb, gps):
    b, t, _ = z2.shape
    da = w0.shape[1]
    wd = gps * GROUP
    ng = da // wd
    nt = t // tb
    per8 = tb // 8

    def zspec(off):
        return pl.BlockSpec((None, tb, wd), lambda bi, g, ti, off=off: (bi, ti, off + g))

    def pspec(off):
        return pl.BlockSpec((None, 8, wd),
                            lambda bi, g, ti, off=off: (bi, jnp.maximum(ti * per8 - 1, 0), off + g))

    def sspec(off):
        return pl.BlockSpec((None, 1, wd), lambda bi, g, ti, off=off: (bi, 0, off + g))

    def vspec(off=0):
        return pl.BlockSpec((1, wd), lambda bi, g, ti, off=off: (0, off + g))

    zl_spec = pl.BlockSpec((None, tb, LANES), lambda bi, g, ti: (bi, ti, 0))
    pl_spec = pl.BlockSpec((None, 8, LANES),
                           lambda bi, g, ti: (bi, jnp.maximum(ti * per8 - 1, 0), 0))
    sl_spec = pl.BlockSpec((None, 1, LANES), lambda bi, g, ti: (bi, 0, 0))
    ml_spec = pl.BlockSpec((1, LANES), lambda bi, g, ti: (0, 0))
    w_spec = pl.BlockSpec((LANES, wd), lambda bi, g, ti: (0, g))

    in_specs = [zspec(0), zspec(ng), zspec(2 * ng), zspec(3 * ng), zl_spec,
                pspec(0), pspec(ng), pspec(2 * ng), pl_spec,
                sspec(0), sspec(ng), sspec(2 * ng), sl_spec,
                vspec(0), vspec(ng), vspec(2 * ng), ml_spec,
                vspec(), vspec(), vspec(), vspec(), vspec(), vspec(), vspec(), w_spec, w_spec]
    blk = lambda: pltpu.VMEM((tb, wd), F32)
    return pl.pallas_call(
        _rwkv_prompt_kernel,
        grid=(b, ng, nt),
        in_specs=in_specs,
        out_specs=[pl.BlockSpec((None, tb, wd), lambda bi, g, ti: (bi, ti, g)),
                   pl.BlockSpec((None, gps * HEADS_PER_GROUP, HD_A, HD_A), lambda bi, g, ti: (bi, g, 0, 0))],
        out_shape=[jax.ShapeDtypeStruct((b, t, da), BF16),
                   jax.ShapeDtypeStruct((b, da // HD_A, HD_A, HD_A), F32)],
        scratch_shapes=[pltpu.VMEM((gps, GROUP, GROUP), F32)] + [blk() for _ in range(8)],
        compiler_params=_cparams(("parallel", "parallel", "arbitrary")),
        name="rwkv_prompt",
    )(z2, z2, z2, z2, zl, z2, z2, z2, zl, sh_rkv, sh_rkv, sh_rkv, sh_l,
      mu_rkv, mu_rkv, mu_rkv, mu_l, w0, a0, k_k, k_a, r_k, gng, gnb, ww_pad, wa_pad)


def _rwkv_dec_pre_kernel(zr, zk, zv, zg, zl, pr, pk, pv, plr, mur, muk, muv, mul,
                         w0, a0, k_k, k_a, r_k, ww, wa,
                         r_o, w_o, k_o, v_o, al_o, be_o, bo_o, sg_o):
    seg = _seg_ones(zr.shape[1], HD_A)
    sh = lambda z, p, mu: z[...] + mu[...] * (p[...] - z[...])
    xr, xk, xv, xl = sh(zr, pr, mur), sh(zk, pk, muk), sh(zv, pv, muv), sh(zl, plr, mul)
    r, k2, v, logw, kk, a, bonus = _rwkv_vectors(
        xr, xk, xv, xl, w0[...], a0[...], k_k[...], k_a[...], r_k[...], ww[...], wa[...], seg)
    r_o[...] = r.T
    w_o[...] = jnp.exp(logw).T
    k_o[...] = k2.T
    v_o[...] = v.T
    al_o[...] = (-kk).T
    be_o[...] = (kk * a).T
    bo_o[...] = bonus
    sg_o[...] = _silu(zg[...])


def _rwkv_dec_pre(z2, zl, prev_rkv, prev_l, mu_rkv, mu_l, w0, a0, k_k, k_a, r_k, ww_pad, wa_pad):
    b, _ = z2.shape
    da = w0.shape[1]
    c = lambda w, j: pl.BlockSpec((b, w), lambda i, j=j: (0, j))
    in_specs = [c(da, 0), c(da, 1), c(da, 2), c(da, 3), c(LANES, 0),
                c(da, 0), c(da, 1), c(da, 2), c(LANES, 0),
                pl.BlockSpec((1, da), lambda i: (0, 0)), pl.BlockSpec((1, da), lambda i: (0, 1)),
                pl.BlockSpec((1, da), lambda i: (0, 2)), pl.BlockSpec((1, LANES), lambda i: (0, 0))]
    in_specs += [pl.BlockSpec((1, da), lambda i: (0, 0))] * 5
    in_specs += [pl.BlockSpec((LANES, da), lambda i: (0, 0))] * 2
    return pl.pallas_call(
        _rwkv_dec_pre_kernel,
        grid=(1,),
        in_specs=in_specs,
        out_specs=[pl.BlockSpec((da, b), lambda i: (0, 0))] * 6 + [pl.BlockSpec((b, da), lambda i: (0, 0))] * 2,
        out_shape=[jax.ShapeDtypeStruct((da, b), F32)] * 6 + [jax.ShapeDtypeStruct((b, da), F32)] * 2,
        compiler_params=_cparams(("arbitrary",)),
        name="rwkv_dec_pre",
    )(z2, z2, z2, z2, zl, prev_rkv, prev_rkv, prev_rkv, prev_l, mu_rkv, mu_rkv, mu_rkv, mu_l,
      w0, a0, k_k, k_a, r_k, ww_pad, wa_pad)


V_ROWS = 8


def _rwkv_dec_state_kernel(s_ref, w_ref, al_ref, be_ref, k_ref, r_ref, v_ref, so_ref, y_ref):
    hb, n, _, _ = s_ref.shape
    for h in range(hb):
        w, al, be, k2, r = w_ref[h], al_ref[h], be_ref[h], k_ref[h], r_ref[h]

        def v_rows(i, carry, h=h, w=w, al=al, be=be, k2=k2, r=r):
            sl = pl.ds(pl.multiple_of(i * V_ROWS, V_ROWS), V_ROWS)
            S = s_ref[h, sl]
            sa = jnp.sum(S * al[None], axis=1)
            vv = v_ref[h, sl, :]
            sn = S * w[None] + sa[:, None, :] * be[None] + vv[:, None, :] * k2[None]
            so_ref[h, sl] = sn
            y_ref[h, sl, :] = jnp.sum(sn * r[None], axis=1)
            return carry

        lax.fori_loop(0, n // V_ROWS, v_rows, 0)


def _rwkv_dec_state(s_hvkb, w, al, be, k2, r, v, hb):
    h, n, _, b = s_hvkb.shape
    sspec = pl.BlockSpec((hb, n, n, b), lambda i: (i, 0, 0, 0))
    rspec = pl.BlockSpec((hb, n, b), lambda i: (i, 0, 0))
    return pl.pallas_call(
        _rwkv_dec_state_kernel,
        grid=(h // hb,),
        in_specs=[sspec] + [rspec] * 6,
        out_specs=[sspec, rspec],
        out_shape=[jax.ShapeDtypeStruct(s_hvkb.shape, F32), jax.ShapeDtypeStruct((h, n, b), F32)],
        compiler_params=_cparams(("parallel",)),
        name="rwkv_dec_state",
    )(s_hvkb, w, al, be, k2, r, v)


def _rotate_pairs(x, cos, sin_signed):
    n = x.shape[-1]
    lane = lax.broadcasted_iota(jnp.int32, x.shape, x.ndim - 1)
    nxt = pltpu.roll(x, n - 1, x.ndim - 1)
    prv = pltpu.roll(x, 1, x.ndim - 1)
    swapped = jnp.where(lane % 2 == 0, nxt, prv)
    return x * cos + swapped * sin_signed


def _head_norm_b(y, g, b):
    mu = jnp.mean(y, axis=-1, keepdims=True)
    yc = y - mu
    var = jnp.mean(yc * yc, axis=-1, keepdims=True)
    return yc * lax.rsqrt(var + GN_EPS_B) * g + b


def _ret_prompt_kernel(lg_ref, q_ref, k_ref, v_ref, g_ref, cos_ref, sin_ref, gng, gnb,
                       y_ref, s_ref, S_scr, dm_scr):
    c = pl.program_id(1)
    nc = pl.num_programs(1)
    L = q_ref.shape[0]
    d = cos_ref.shape[1]
    nh = q_ref.shape[1] // d

    @pl.when(c == 0)
    def _():
        S_scr[...] = jnp.zeros_like(S_scr)
        ii = lax.broadcasted_iota(jnp.int32, (L, L), 0)
        jj = lax.broadcasted_iota(jnp.int32, (L, L), 1)
        diff = (ii - jj).astype(F32)
        for h in range(nh):
            lg = lg_ref[h, 0:1, 0:1]
            dm_scr[h] = jnp.where(diff >= 0, jnp.exp(jnp.maximum(diff, 0.0) * lg), 0.0)

    cos, sin = cos_ref[...], sin_ref[...]
    idx = lax.broadcasted_iota(jnp.int32, (L, 1), 0).astype(F32)

    def head(h):
        ln = slice(h * d, (h + 1) * d)
        lg = lg_ref[h, 0:1, 0:1]
        q = _rotate_pairs(q_ref[:, ln], cos, sin)
        k = _rotate_pairs(k_ref[:, ln], cos, sin) * (d ** -0.5)
        v = v_ref[:, ln]
        S = S_scr[h]
        yield
        scores = _bdot_nt(q, k) * dm_scr[h]
        cross = _bdot(q, S) * jnp.exp((idx + 1.0) * lg)
        yield
        inner = _bdot(scores, v)
        kdec = jnp.exp((L - 1.0 - idx) * lg)
        S_scr[h] = S * jnp.exp(L * lg) + _bdot_tn(k * kdec, v)
        yield
        y = _head_norm_b(inner + cross, gng[:, ln], gnb[:, ln]) * _silu(g_ref[:, ln])
        y_ref[:, ln] = y.astype(y_ref.dtype)

    active = [head(h) for h in range(nh)]
    while active:
        active = [gen for gen in active if next(gen, True) is None]

    @pl.when(c == nc - 1)
    def _():
        s_ref[...] = S_scr[...]


def _ret_prompt(z2, lg_tile, cos, sin, gng, gnb, col0):
    b, t, _ = z2.shape
    d = cos.shape[1]
    db = gng.shape[1]
    nh = db // d
    L = _pick_tile(t, (CHUNK_B,))
    nchunk = t // L
    c0 = col0 // db
    zspec = lambda off: pl.BlockSpec((None, L, db), lambda bi, ci, off=off: (bi, ci, c0 + off))
    tspec = pl.BlockSpec((L, d), lambda bi, ci: (ci, 0))
    vspec = pl.BlockSpec((1, db), lambda bi, ci: (0, 0))
    return pl.pallas_call(
        _ret_prompt_kernel,
        grid=(b, nchunk),
        in_specs=[pl.BlockSpec(lg_tile.shape, lambda bi, ci: (0, 0, 0)),
                  zspec(0), zspec(1), zspec(2), zspec(3), tspec, tspec, vspec, vspec],
        out_specs=[pl.BlockSpec((None, L, db), lambda bi, ci: (bi, ci, 0)),
                   pl.BlockSpec((None, nh, d, d), lambda bi, ci: (bi, 0, 0, 0))],
        out_shape=[jax.ShapeDtypeStruct((b, t, db), BF16),
                   jax.ShapeDtypeStruct((b, nh, d, d), F32)],
        scratch_shapes=[pltpu.VMEM((nh, d, d), F32), pltpu.VMEM((nh, L, L), F32)],
        compiler_params=_cparams(("parallel", "arbitrary")),
        name="ret_prompt",
    )(lg_tile, z2, z2, z2, z2, cos, sin, gng, gnb)


def _ret_dec_pre_kernel(zq, zk, zv, zg, cos_ref, sin_ref, q_o, k_o, in_o, sg_o):
    d = cos_ref.shape[1] // H_B
    seg = _seg_ones(zq.shape[1], d)
    cos, sin = cos_ref[...], sin_ref[...]
    q = _rotate_pairs(zq[...], cos, sin)
    k = _rotate_pairs(zk[...], cos, sin) * (d ** -0.5)
    q_o[...] = q
    k_o[...] = k
    in_o[...] = _dot_exact_rhs(q * k, seg) * zv[...]
    sg_o[...] = _silu(zg[...])


def _ret_dec_pre(z2, cos, sin, col0):
    b, _ = z2.shape
    db = cos.shape[1]
    c0 = col0 // db
    c = lambda j: pl.BlockSpec((b, db), lambda i, j=j: (0, c0 + j))
    t = pl.BlockSpec((1, db), lambda i: (0, 0))
    return pl.pallas_call(
        _ret_dec_pre_kernel,
        grid=(1,),
        in_specs=[c(0), c(1), c(2), c(3), t, t],
        out_specs=[pl.BlockSpec((b, db), lambda i: (0, 0))] * 4,
        out_shape=[jax.ShapeDtypeStruct((b, db), F32)] * 4,
        compiler_params=_cparams(("arbitrary",)),
        name="ret_dec_pre",
    )(z2, z2, z2, z2, cos, sin)


def _ret_dec_state_kernel(lg_ref, s_ref, q_ref, k_ref, v_ref, so_ref, cr_ref):
    bb, nh, d, _ = s_ref.shape
    rowid = lax.broadcasted_iota(jnp.int32, (bb, 1), 0)
    for h in range(nh):
        ln = slice(h * d, (h + 1) * d)
        gamma = jnp.exp(lg_ref[h, 0:1, 0:1])
        q, k, v = q_ref[:, ln], k_ref[:, ln], v_ref[:, ln]
        cross = jnp.zeros((bb, d), F32)
        for b in range(bb):
            sel = rowid == b
            S = s_ref[b, h]
            cross = cross + _bdot(jnp.where(sel, q, 0.0), S)
            so_ref[b, h] = S * gamma + _bdot_tn(jnp.where(sel, k, 0.0), v)
        cr_ref[:, ln] = cross * gamma


def _ret_dec_state(lg_tile, s0, q, k, v, bb):
    b, nh, d, _ = s0.shape
    sspec = pl.BlockSpec((bb, nh, d, d), lambda i: (i, 0, 0, 0))
    rspec = pl.BlockSpec((bb, nh * d), lambda i: (i, 0))
    return pl.pallas_call(
        _ret_dec_state_kernel,
        grid=(b // bb,),
        in_specs=[pl.BlockSpec(lg_tile.shape, lambda i: (0, 0, 0)), sspec, rspec, rspec, rspec],
        out_specs=[sspec, rspec],
        out_shape=[jax.ShapeDtypeStruct(s0.shape, F32), jax.ShapeDtypeStruct((b, nh * d), F32)],
        compiler_params=_cparams(("parallel",)),
        name="ret_dec_state",
    )(lg_tile, s0, q, k, v)


def _dec_post_kernel(ya_ref, bo_ref, sga_ref, gag, gab, in_ref, cr_ref, sgb_ref, gbg, gbb,
                     oa_ref, ob_ref):
    da = ya_ref.shape[0]
    db = in_ref.shape[1]
    d = db // H_B
    sega = _seg_ones(da, HD_A)
    segb = _seg_ones(db, d)
    y = ya_ref[...].T
    mu = _dot_exact_rhs(y, sega) * (1.0 / HD_A)
    yc = y - mu
    var = _dot_exact_rhs(yc * yc, sega) * (1.0 / HD_A)
    hn = yc * lax.rsqrt(var + GN_EPS_A) * gag[...] + gab[...]
    oa_ref[...] = ((hn + bo_ref[...]) * sga_ref[...]).astype(oa_ref.dtype)
    y = in_ref[...] + cr_ref[...]
    mu = _dot_exact_rhs(y, segb) * (1.0 / d)
    yc = y - mu
    var = _dot_exact_rhs(yc * yc, segb) * (1.0 / d)
    hn = yc * lax.rsqrt(var + GN_EPS_B) * gbg[...] + gbb[...]
    ob_ref[...] = (hn * sgb_ref[...]).astype(ob_ref.dtype)


def _dec_post(ya_t, bonus, sga, gag, gab, inner, cross, sgb, gbg, gbb):
    da, b = ya_t.shape
    db = inner.shape[1]
    ft = pl.BlockSpec((da, b), lambda i: (0, 0))
    fa = pl.BlockSpec((b, da), lambda i: (0, 0))
    fb = pl.BlockSpec((b, db), lambda i: (0, 0))
    va = pl.BlockSpec((1, da), lambda i: (0, 0))
    vb = pl.BlockSpec((1, db), lambda i: (0, 0))
    return pl.pallas_call(
        _dec_post_kernel,
        grid=(1,),
        in_specs=[ft, fa, fa, va, va, fb, fb, fb, vb, vb],
        out_specs=[fa, fb],
        out_shape=[jax.ShapeDtypeStruct((b, da), BF16), jax.ShapeDtypeStruct((b, db), BF16)],
        compiler_params=_cparams(("arbitrary",)),
        name="dec_post",
    )(ya_t, bonus, sga, gag, gab, inner, cross, sgb, gbg, gbb)


def _rot_tables(pos, d):
    angle = 1.0 / (ROPE_BASE ** jnp.linspace(0.0, 1.0, d // 2, dtype=F32))
    theta = pos[:, None] * angle[None, :]
    cos = jnp.repeat(jnp.cos(theta), 2, axis=-1)
    sin = jnp.repeat(jnp.sin(theta), 2, axis=-1)
    sign = jnp.tile(jnp.array([-1.0, 1.0], F32), d // 2)
    return cos, sin * sign


def _pick_tile(n, pref):
    for t in pref:
        if n % t == 0:
            return t
    return n


def kernel(x_prompt, x_sample, p_prompt, p_sample, state_wkv, state_shift, state_ret, g_ln, w_in,
           mu_shift, w0, w_wB, a0, w_aB, k_k, k_a, r_k, gn_a_g, gn_a_b, gn_b_g, gn_b_b, w_out, w_ple,
           w_ple_gate, g_final):
    depth = g_ln.shape[0]
    assert depth == 1
    bp, tp, d_model = x_prompt.shape
    bs, ts, _ = x_sample.shape
    assert ts == 1
    da = w0.shape[1]
    db = gn_b_g.shape[1]
    hd_b = db // H_B
    n_rkv = 3 * da
    a_shift = n_rkv + 2 * LORA
    assert LORA * 2 == LANES and a_shift == mu_shift.shape[1]
    assert tp % CHUNK_A == 0 and da % GROUP == 0

    w = w_in[0]
    w_all = _wprep(w, n_rkv, a_shift, _pick_tile(d_model, (256, 128)))
    col_b = n_rkv + da
    mu = mu_shift[0]
    mu_rkv = mu[:n_rkv].reshape(1, n_rkv)
    mu_l = mu[n_rkv:].reshape(1, LANES)
    zpad = jnp.zeros((LORA, da), F32)
    ww_pad = jnp.concatenate([w_wB[0], zpad], axis=0).astype(BF16)
    wa_pad = jnp.concatenate([zpad, w_aB[0]], axis=0).astype(BF16)
    v1 = lambda a: a[0].reshape(1, -1)
    w0_, a0_, kk_, ka_, rk_ = v1(w0), v1(a0), v1(k_k), v1(k_a), v1(r_k)
    gag, gab, gbg, gbb = v1(gn_a_g), v1(gn_a_b), v1(gn_b_g), v1(gn_b_b)
    woa = w_out[0][:da].astype(BF16)
    wob = w_out[0][da:].astype(BF16)
    wg = w_ple_gate[0].astype(BF16)
    wp = w_ple[0].astype(BF16)
    lg = jnp.log(1.0 - jnp.exp2(-5.0 - jnp.arange(H_B, dtype=F32)))
    lg_tile = jnp.broadcast_to(lg[:, None, None], (H_B, 8, LANES))

    tn = _pick_tile(w_all.shape[1] - LANES, (1024, 512, 256, 128))

    xp = x_prompt.reshape(bp * tp, d_model)
    zp, zlp = _proj(xp, g_ln[0], w_all, LANES, _pick_tile(bp * tp, (1024, 512, 256, 128)), tn)
    zp3 = zp.reshape(bp, tp, -1)
    zlp3 = zlp.reshape(bp, tp, LANES)
    sh0 = jnp.zeros((bp, 1, a_shift), F32)
    tb = _pick_tile(tp, (256, 128, 64))
    ya_p, wkv_p = _rwkv_prompt(zp3, zlp3, sh0[:, :, :n_rkv], sh0[:, :, n_rkv:], mu_rkv, mu_l, w0_, a0_, kk_,
                              ka_, rk_, gag, gab, ww_pad, wa_pad, tb, da // GROUP)
    cos_p, sin_p = _rot_tables(jnp.arange(tp, dtype=F32), hd_b)
    yb_p, ret_p = _ret_prompt(zp3, lg_tile, cos_p, sin_p, gbg, gbb, col_b)
    y_prompt = _out_block(xp, ya_p.reshape(bp * tp, da), yb_p.reshape(bp * tp, db),
                          p_prompt[0].reshape(bp * tp, -1), woa, wob, wg, wp, g_final,
                          _pick_tile(bp * tp, (256, 128))).reshape(bp, tp, d_model)
    shift_p = jnp.concatenate([zp3[:, -1, :n_rkv], zlp3[:, -1]], axis=-1)

    xs = x_sample.reshape(bs, d_model)
    zs, zls = _proj(xs, g_ln[0], w_all, LANES, bs, tn)
    prev = state_shift[0]
    r, wdec, k2, v, al, be, bonus, sga = _rwkv_dec_pre(
        zs, zls, prev[:, :n_rkv], prev[:, n_rkv:], mu_rkv, mu_l, w0_, a0_, kk_, ka_, rk_, ww_pad, wa_pad)
    h_a = da // HD_A
    hs = lambda a: a.reshape(h_a, HD_A, bs)
    s_hvkb = jnp.transpose(state_wkv[0], (1, 2, 3, 0))
    s_new, y_t = _rwkv_dec_state(s_hvkb, hs(wdec), hs(al), hs(be), hs(k2), hs(r), hs(v),
                                 _pick_tile(h_a, (2, 1)))
    wkv_s = jnp.transpose(s_new, (3, 0, 1, 2))
    cos_s, sin_s = _rot_tables((PAST_LEN + jnp.arange(ts)).astype(F32), hd_b)
    cos_s = jnp.tile(cos_s, (1, H_B))
    sin_s = jnp.tile(sin_s, (1, H_B))
    q_s, k_s, inner, sgb = _ret_dec_pre(zs, cos_s, sin_s, col_b)
    ret_s, cross = _ret_dec_state(lg_tile, state_ret[0], q_s, k_s,
                                  zs[:, col_b + 2 * db:col_b + 3 * db], _pick_tile(bs, (8,)))
    ya_s, yb_s = _dec_post(y_t.reshape(da, bs), bonus, sga, gag, gab, inner, cross, sgb, gbg, gbb)
    y_sample = _out_block(xs, ya_s, yb_s, p_sample[0].reshape(bs, -1), woa, wob, wg, wp, g_final,
                          bs).reshape(bs, 1, d_model)
    shift_s = jnp.concatenate([zs[:, :n_rkv], zls], axis=-1)

    return (y_prompt, y_sample, wkv_p[None], shift_p[None], ret_p[None],
            wkv_s[None], shift_s[None], ret_s[None])
```
